```python
import math
import jax, jax.numpy as jnp
from jax import lax
import numpy as np

D_MODEL = 1024
BATCH = 8
SEQ = 2048
DEPTH = 2

EPS = 1e-6
MEM_LEN = 256
N_BRANCH = 4

A_WIDTH = 256
A_GROUPS = 4
A_GROUP_DIM = A_WIDTH // A_GROUPS
A_CHUNK = 128

B_HEADS = 8
B_NOPE = 64
B_ROPE = 32
B_QK_DIM = B_NOPE + B_ROPE
B_VDIM = 64
B_Q_RANK = 768
B_KV_RANK = 256
B_WIDTH = B_HEADS * B_VDIM
ATTN_BLOCK = 128
ROPE_THETA = 10000.0

C_WIDTH = 256
C_GROUP = 16
C_GROUPS = C_WIDTH // C_GROUP
C_STATE = 64

M_HEADS = 4
M_HEAD_DIM = 64
M_WIDTH = M_HEADS * M_HEAD_DIM

IN_SIZES = (A_WIDTH, A_WIDTH, A_WIDTH,
            B_Q_RANK, B_KV_RANK, B_ROPE, B_WIDTH,
            C_WIDTH, C_WIDTH,
            M_WIDTH, M_WIDTH,
            N_BRANCH * D_MODEL)
IN_WIDTH = sum(IN_SIZES)

kernel_name = "hybrid_gated_gmlp_mla_s5_memxattn"


def rms_norm(x, g):
    xf = x.astype(jnp.float32)
    y = xf * lax.rsqrt(jnp.mean(xf * xf, axis=-1, keepdims=True) + EPS)
    return (y * g.astype(jnp.float32)).astype(x.dtype)


def apply_rope(x, positions):
    half = x.shape[-1] // 2
    inv_freq = ROPE_THETA ** (-jnp.arange(half, dtype=jnp.float32) / half)
    ang = positions.astype(jnp.float32)[..., None] * inv_freq
    cos = jnp.cos(ang)[:, :, None, :]
    sin = jnp.sin(ang)[:, :, None, :]
    xf = x.astype(jnp.float32)
    x1, x2 = xf[..., :half], xf[..., half:]
    out = jnp.concatenate([x1 * cos - x2 * sin, x2 * cos + x1 * sin], axis=-1)
    return out.astype(x.dtype)


def chunked_spatial_gating(u, v, g_v, w_s, b_s):
    bsz, seq, _ = v.shape
    n_chunks = seq // A_CHUNK
    v = rms_norm(v, g_v)
    vc = v.reshape(bsz, n_chunks, A_CHUNK, A_GROUPS, A_GROUP_DIM)
    causal = jnp.tril(jnp.ones((A_CHUNK, A_CHUNK), dtype=bool))
    w = jnp.where(causal, w_s, jnp.zeros((), w_s.dtype))
    s = jnp.einsum('gts,bnsgc->bntgc', w, vc) + b_s.T[None, None, :, :, None]
    return u * s.reshape(bsz, seq, A_WIDTH)


def causal_block_attention(q, k, v):
    bsz, seq, n_heads, dqk = q.shape
    dv = v.shape[-1]
    n_blocks = seq // ATTN_BLOCK
    scale = dqk ** -0.5
    qb = q.reshape(bsz, n_blocks, ATTN_BLOCK, n_heads, dqk).transpose(1, 0, 3, 2, 4)
    key_pos = jnp.arange(seq)

    def one_block(args):
        q_blk, blk = args
        s = jnp.einsum('bhtd,bshd->bhts', q_blk, k).astype(jnp.float32) * scale
        query_pos = blk * ATTN_BLOCK + jnp.arange(ATTN_BLOCK)
        s = jnp.where(key_pos[None, :] <= query_pos[:, None], s, -jnp.inf)
        p = jax.nn.softmax(s, axis=-1).astype(v.dtype)
        return jnp.einsum('bhts,bshd->bthd', p, v)

    out = lax.map(one_block, (qb, jnp.arange(n_blocks)))
    return out.transpose(1, 0, 2, 3, 4).reshape(bsz, seq, n_heads * dv)


def latent_attention(c_q, c_kv, k_rope, positions, q_norm_g, kv_norm_g, w_uq, w_ukv,
                     qk_g_q, qk_g_k):
    bsz, seq, _ = c_q.shape
    q = (rms_norm(c_q, q_norm_g) @ w_uq).reshape(bsz, seq, B_HEADS, B_QK_DIM)
    kv = (rms_norm(c_kv, kv_norm_g) @ w_ukv).reshape(bsz, seq, B_HEADS, B_NOPE + B_VDIM)
    k_nope, v = kv[..., :B_NOPE], kv[..., B_NOPE:]
    k_pe = jnp.broadcast_to(k_rope[:, :, None, :], (bsz, seq, B_HEADS, B_ROPE))
    k = jnp.concatenate([k_nope, k_pe], axis=-1)
    q = rms_norm(q, qk_g_q)
    k = rms_norm(k, qk_g_k)
    q = jnp.concatenate([q[..., :B_NOPE], apply_rope(q[..., B_NOPE:], positions)], axis=-1)
    k = jnp.concatenate([k[..., :B_NOPE], apply_rope(k[..., B_NOPE:], positions)], axis=-1)
    return causal_block_attention(q, k, v)


def s5_layer(u, a_re, a_im, log_dt, b_re, b_im, c_re, c_im, d_skip, w_glu, b_glu):
    bsz, seq, _ = u.shape
    f32 = jnp.float32
    uf = u.astype(f32).reshape(bsz, seq, C_GROUPS, C_GROUP)
    lam = lax.complex(a_re.astype(f32), a_im.astype(f32))
    dt = jnp.exp(log_dt.astype(f32))[:, None]
    a_bar = jnp.exp(lam * dt)
    b_mat = lax.complex(b_re.astype(f32), b_im.astype(f32))
    b_bar = ((a_bar - 1.0) / lam)[..., None] * b_mat
    bu = jnp.einsum('gpc,bsgc->bsgp', b_bar, uf.astype(jnp.complex64))
    a_elems = jnp.broadcast_to(a_bar, bu.shape)

    def combine(left, right):
        a_l, b_l = left
        a_r, b_r = right
        return a_r * a_l, a_r * b_l + b_r

    _, states = lax.associative_scan(combine, (a_elems, bu), axis=1)
    c_mat = lax.complex(c_re.astype(f32), c_im.astype(f32))
    y = jnp.einsum('gcp,bsgp->bsgc', c_mat, states).real
    y = y + d_skip.astype(f32).reshape(C_GROUPS, C_GROUP) * uf
    y = jax.nn.gelu(y.reshape(bsz, seq, C_WIDTH))
    y = y * jax.nn.sigmoid(y @ w_glu.astype(f32) + b_glu.astype(f32))
    return y.astype(u.dtype)


def memory_attention(q, mem_h, w_kv, qk_g_q, qk_g_k):
    bsz, seq, _ = q.shape
    kv = mem_h @ w_kv
    k = kv[..., :M_WIDTH].reshape(bsz, -1, M_HEADS, M_HEAD_DIM)
    v = kv[..., M_WIDTH:].reshape(bsz, -1, M_HEADS, M_HEAD_DIM)
    q = rms_norm(q.reshape(bsz, seq, M_HEADS, M_HEAD_DIM), qk_g_q)
    k = rms_norm(k, qk_g_k)
    s = jnp.einsum('bshd,bmhd->bhsm', q, k).astype(jnp.float32) * (M_HEAD_DIM ** -0.5)
    p = jax.nn.softmax(s, axis=-1).astype(v.dtype)
    return jnp.einsum('bhsm,bmhd->bshd', p, v).reshape(bsz, seq, M_WIDTH)


def hybrid_layer(x, mem, positions, norm_g, w_in, b_merge, a_norm_g, a_w_s, a_b_s,
                 b_q_norm_g, b_kv_norm_g, b_w_uq, b_w_ukv, b_qk_g_q, b_qk_g_k,
                 c_a_re, c_a_im, c_log_dt, c_b_re, c_b_im, c_c_re, c_c_im, c_d,
                 c_w_glu, c_b_glu, m_norm_g, m_w_kv, m_qk_g_q, m_qk_g_k,
                 w_br_a, w_br_b, w_br_c, w_br_m, w_out):
    bsz, seq, _ = x.shape
    h = rms_norm(x, norm_g)
    z = h @ w_in
    splits = [int(s) for s in np.cumsum(IN_SIZES)[:-1]]
    (a_u, a_v, a_gate, b_cq, b_ckv, b_krope, b_gate, c_in, c_gate,
     m_q, m_gate, merge_logits) = jnp.split(z, splits, axis=-1)

    y_a = chunked_spatial_gating(jax.nn.gelu(a_u), jax.nn.gelu(a_v), a_norm_g, a_w_s, a_b_s)
    y_a = y_a * jax.nn.silu(a_gate)
    y_b = latent_attention(b_cq, b_ckv, b_krope, positions, b_q_norm_g, b_kv_norm_g,
                           b_w_uq, b_w_ukv, b_qk_g_q, b_qk_g_k) * jax.nn.silu(b_gate)
    y_c = s5_layer(c_in, c_a_re, c_a_im, c_log_dt, c_b_re, c_b_im, c_c_re, c_c_im,
                   c_d, c_w_glu, c_b_glu) * jax.nn.silu(c_gate)
    y_m = memory_attention(m_q, rms_norm(mem, m_norm_g), m_w_kv, m_qk_g_q, m_qk_g_k)
    y_m = y_m * jax.nn.silu(m_gate)

    gates = jax.nn.sigmoid(merge_logits + b_merge).reshape(bsz, seq, N_BRANCH, D_MODEL)
    merged = (gates[:, :, 0] * (y_a @ w_br_a) + gates[:, :, 1] * (y_b @ w_br_b)
              + gates[:, :, 2] * (y_c @ w_br_c) + gates[:, :, 3] * (y_m @ w_br_m))
    return x + merged @ w_out


def setup_inputs(seed: int = 0) -> dict:
    key = jax.random.key(seed)
    ks = iter(jax.random.split(key, 40))
    L, D = DEPTH, D_MODEL

    def nrm(shape, scale):
        return jax.random.normal(next(ks), shape, jnp.float32) * scale

    def gain(shape):
        return 1.0 + nrm(shape, 0.02)

    x = nrm((BATCH, SEQ, D), 1.0)
    mem = nrm((BATCH, MEM_LEN, D), 1.0)
    offsets = jax.random.randint(next(ks), (BATCH, 1), 0, 4096, dtype=jnp.int32)
    positions = jnp.arange(SEQ, dtype=jnp.int32)[None, :] + offsets

    state_idx = jnp.arange(C_STATE, dtype=jnp.float32)
    return {
        "x": x,
        "mem": mem,
        "positions": positions,
        "norm_g": gain((L, D)),
        "w_in": nrm((L, D, IN_WIDTH), D ** -0.5),
        "b_merge": nrm((L, N_BRANCH * D), 0.02),
        "a_norm_g": gain((L, A_WIDTH)),
        "a_w_s": nrm((L, A_GROUPS, A_CHUNK, A_CHUNK), A_CHUNK ** -0.5),
        "a_b_s": 1.0 + nrm((L, A_GROUPS, A_CHUNK), 0.1),
        "b_q_norm_g": gain((L, B_Q_RANK)),
        "b_kv_norm_g": gain((L, B_KV_RANK)),
        "b_w_uq": nrm((L, B_Q_RANK, B_HEADS * B_QK_DIM), B_Q_RANK ** -0.5),
        "b_w_ukv": nrm((L, B_KV_RANK, B_HEADS * (B_NOPE + B_VDIM)), B_KV_RANK ** -0.5),
        "b_qk_g_q": gain((L, B_QK_DIM)),
        "b_qk_g_k": gain((L, B_QK_DIM)),
        "c_a_re": -0.5 + nrm((L, C_GROUPS, C_STATE), 0.01),
        "c_a_im": jnp.broadcast_to(math.pi * state_idx, (L, C_GROUPS, C_STATE)),
        "c_log_dt": jax.random.uniform(next(ks), (L, C_GROUPS), jnp.float32,
                                       math.log(1e-3), math.log(1e-1)),
        "c_b_re": nrm((L, C_GROUPS, C_STATE, C_GROUP), (2.0 * C_GROUP) ** -0.5),
        "c_b_im": nrm((L, C_GROUPS, C_STATE, C_GROUP), (2.0 * C_GROUP) ** -0.5),
        "c_c_re": nrm((L, C_GROUPS, C_GROUP, C_STATE), (2.0 * C_STATE) ** -0.5),
        "c_c_im": nrm((L, C_GROUPS, C_GROUP, C_STATE), (2.0 * C_STATE) ** -0.5),
        "c_d": nrm((L, C_WIDTH), 1.0),
        "c_w_glu": nrm((L, C_WIDTH, C_WIDTH), C_WIDTH ** -0.5),
        "c_b_glu": nrm((L, C_WIDTH), 0.02),
        "m_norm_g": gain((L, D)),
        "m_w_kv": nrm((L, D, 2 * M_WIDTH), D ** -0.5),
        "m_qk_g_q": gain((L, M_HEAD_DIM)),
        "m_qk_g_k": gain((L, M_HEAD_DIM)),
        "w_br_a": nrm((L, A_WIDTH, D), A_WIDTH ** -0.5),
        "w_br_b": nrm((L, B_WIDTH, D), B_WIDTH ** -0.5),
        "w_br_c": nrm((L, C_WIDTH, D), C_WIDTH ** -0.5),
        "w_br_m": nrm((L, M_WIDTH, D), M_WIDTH ** -0.5),
        "w_out": nrm((L, D, D), D ** -0.5),
    }


def reference(x, mem, positions, norm_g, w_in, b_merge, a_norm_g, a_w_s, a_b_s,
              b_q_norm_g, b_kv_norm_g, b_w_uq, b_w_ukv, b_qk_g_q, b_qk_g_k,
              c_a_re, c_a_im, c_log_dt, c_b_re, c_b_im, c_c_re, c_c_im, c_d,
              c_w_glu, c_b_glu, m_norm_g, m_w_kv, m_qk_g_q, m_qk_g_k,
              w_br_a, w_br_b, w_br_c, w_br_m, w_out):
    for l in range(DEPTH):
        x = hybrid_layer(x, mem, positions, norm_g[l], w_in[l], b_merge[l],
                         a_norm_g[l], a_w_s[l], a_b_s[l],
                         b_q_norm_g[l], b_kv_norm_g[l], b_w_uq[l], b_w_ukv[l],
                         b_qk_g_q[l], b_qk_g_k[l],
                         c_a_re[l], c_a_im[l], c_log_dt[l], c_b_re[l], c_b_im[l],
                         c_c_re[l], c_c_im[l], c_d[l], c_w_glu[l], c_b_glu[l],
                         m_norm_g[l], m_w_kv[l], m_qk_g_q[l], m_qk_g_k[l],
                         w_br_a[l], w_br_b[l], w_br_c[l], w_br_m[l], w_out[l])
    return x
```

```python
import functools
import math

import jax
import jax.numpy as jnp
from jax import lax
from jax.experimental import pallas as pl
from jax.experimental.pallas import tpu as pltpu

F32 = jnp.float32
BF16 = jnp.bfloat16

D_MODEL = 1024
EPS = 1e-6
N_BRANCH = 4

A_WIDTH = 256
A_GROUPS = 4
A_GROUP_DIM = A_WIDTH // A_GROUPS
A_CHUNK = 128

B_HEADS = 8
B_NOPE = 64
B_ROPE = 32
B_QK_DIM = B_NOPE + B_ROPE
B_VDIM = 64
B_Q_RANK = 768
B_KV_RANK = 256
B_WIDTH = B_HEADS * B_VDIM
ROPE_THETA = 10000.0
HEAD_PAD = 128
QK_PAD = B_HEADS * HEAD_PAD

C_WIDTH = 256
C_GROUP = 16
C_GROUPS = C_WIDTH // C_GROUP
C_STATE = 64
C_STATES = C_GROUPS * C_STATE

M_HEADS = 4
M_HEAD_DIM = 64
M_WIDTH = M_HEADS * M_HEAD_DIM

COL_AU, COL_AV, COL_AG = 0, 256, 512
COL_CQ = 768
COL_CKV = 1536
COL_KPE = 1792
COL_BG = 1920
COL_CI = 2432
COL_CG = 2688
COL_MQ = 2944
COL_MG = 3200
N_SMALL = 3456

V7X_VMEM_LIMIT_BYTES = 56 * 1024 * 1024

TOKEN_TILE = 256
ATTN_TILE = 256
S5_CHUNK = 64
S5_LANE_BLOCK = 512


def _sigmoid(x):
    return 1.0 / (1.0 + jnp.exp(-x))


def _silu(x):
    return x * _sigmoid(x)


def _rms(x, g):
    return x * lax.rsqrt(jnp.mean(x * x, axis=-1, keepdims=True) + EPS) * g


def _dot(a, b):
    return jnp.dot(a, b, preferred_element_type=F32)


def _dot_nt(a, b):
    return lax.dot_general(a, b, (((1,), (1,)), ((), ())), preferred_element_type=F32)


def _rope_table_body(pos_ref, freq_ref, c_ref, sa_ref, sb_ref):
    ang = pos_ref[...] * freq_ref[...]
    lane = lax.broadcasted_iota(jnp.int32, ang.shape, 1)
    first = (lane >= B_NOPE) & (lane < B_NOPE + B_ROPE // 2)
    second = (lane >= B_NOPE + B_ROPE // 2) & (lane < B_QK_DIM)
    cos = jnp.cos(ang)
    sin = jnp.sin(ang)
    c_ref[...] = jnp.where(first | second, cos, 1.0)
    sa_ref[...] = jnp.where(second, sin, 0.0)
    sb_ref[...] = jnp.where(first, -sin, 0.0)


def _rope_tables(positions):
    n_tok = positions.size
    half = B_ROPE // 2
    inv_freq = ROPE_THETA ** (-jnp.arange(half, dtype=F32) / half)
    freq_row = jnp.zeros((1, HEAD_PAD), F32)
    freq_row = freq_row.at[0, B_NOPE:B_NOPE + half].set(inv_freq)
    freq_row = freq_row.at[0, B_NOPE + half:B_QK_DIM].set(inv_freq)
    pos = positions.astype(F32).reshape(n_tok, 1)
    tm = 1024
    tab = jax.ShapeDtypeStruct((n_tok, HEAD_PAD), F32)
    return pl.pallas_call(
        _rope_table_body,
        out_shape=(tab, tab, tab),
        grid=(n_tok // tm,),
        in_specs=[pl.BlockSpec((tm, 1), lambda i: (i, 0)),
                  pl.BlockSpec((1, HEAD_PAD), lambda i: (0, 0))],
        out_specs=tuple(pl.BlockSpec((tm, HEAD_PAD), lambda i: (i, 0)) for _ in range(3)),
        compiler_params=pltpu.CompilerParams(dimension_semantics=("parallel",)),
        name="rope_tables",
    )(pos, freq_row)


def _rope(x, c, sa, sb):
    return x * c + pltpu.roll(x, 16, 1) * sa + pltpu.roll(x, HEAD_PAD - 16, 1) * sb


def _inproj_body(x_ref, ng_ref, w_ref, ang_ref, wst_ref, bias_ref, qng_ref, kvng_ref,
                 wuq_ref, wuk_ref, wuv_ref, gq_ref, gk_ref, c_ref, sa_ref, sb_ref,
                 ya_ref, q_ref, k_ref, v_ref, bg_ref, ci_ref, cg_ref, mq_ref, mg_ref):
    tm = x_ref.shape[0]
    h = _rms(x_ref[...], ng_ref[...]).astype(BF16)

    def proj(lo, hi):
        return _dot(h, w_ref[:, lo:hi])

    bg_ref[...] = _silu(proj(COL_BG, COL_CI))
    ci_ref[...] = proj(COL_CI, COL_CG)
    cg_ref[...] = _silu(proj(COL_CG, COL_MQ))
    mq_ref[...] = proj(COL_MQ, COL_MG)
    mg_ref[...] = _silu(proj(COL_MG, N_SMALL))

    u = jax.nn.gelu(proj(COL_AU, COL_AV))
    v = _rms(jax.nn.gelu(proj(COL_AV, COL_AG)), ang_ref[...]).astype(BF16)
    ag = _silu(proj(COL_AG, COL_CQ))
    lane_group = lax.broadcasted_iota(jnp.int32, (A_CHUNK, A_WIDTH), 1) // A_GROUP_DIM
    for c in range(tm // A_CHUNK):
        rows = slice(c * A_CHUNK, (c + 1) * A_CHUNK)
        s_all = _dot(wst_ref[...], v[rows, :])
        s = s_all[(A_GROUPS - 1) * A_CHUNK:, :]
        for g in range(A_GROUPS - 2, -1, -1):
            s = jnp.where(lane_group == g, s_all[g * A_CHUNK:(g + 1) * A_CHUNK, :], s)
        s = s + bias_ref[...]
        ya_ref[rows, :] = (u[rows, :] * s * ag[rows, :]).astype(BF16)

    c_tab, sa_tab, sb_tab = c_ref[...], sa_ref[...], sb_ref[...]
    q_scale = B_QK_DIM ** -0.5
    cq = _rms(proj(COL_CQ, COL_CKV), qng_ref[...]).astype(BF16)
    qf = _dot(cq, wuq_ref[...])
    for hd in range(B_HEADS):
        lanes = slice(hd * HEAD_PAD, (hd + 1) * HEAD_PAD)
        qh = qf[:, lanes]
        ssq = jnp.sum(qh * qh, axis=-1, keepdims=True)
        qh = qh * lax.rsqrt(ssq * (1.0 / B_QK_DIM) + EPS) * gq_ref[...]
        q_ref[:, lanes] = (_rope(qh, c_tab, sa_tab, sb_tab) * q_scale).astype(BF16)

    ckv = _rms(proj(COL_CKV, COL_KPE), kvng_ref[...]).astype(BF16)
    v_ref[...] = _dot(ckv, wuv_ref[...]).astype(BF16)
    kf = _dot(ckv, wuk_ref[...])
    kpe = proj(COL_KPE, COL_BG)
    for hd in range(B_HEADS):
        lanes = slice(hd * HEAD_PAD, (hd + 1) * HEAD_PAD)
        kh = kf[:, lanes] + kpe
        ssq = jnp.sum(kh * kh, axis=-1, keepdims=True)
        kh = kh * lax.rsqrt(ssq * (1.0 / B_QK_DIM) + EPS) * gk_ref[...]
        k_ref[:, lanes] = _rope(kh, c_tab, sa_tab, sb_tab).astype(BF16)


def _inproj(x2, p, tabs, bsz, seq):
    n_tok = x2.shape[0]
    tm = TOKEN_TILE
    nst = seq // tm

    def tok(i):
        return (i, 0)

    def seq_major(i):
        return (i % nst, i // nst)

    def const(i):
        return (0, 0)

    def full(a):
        return pl.BlockSpec(a.shape, const)

    weights = (p["norm_g"], p["w_small"], p["a_norm_g"], p["a_wst"], p["a_bias"],
               p["b_q_norm_g"], p["b_kv_norm_g"], p["b_wuq"], p["b_wuk"], p["b_wuv"],
               p["b_gq"], p["b_gk"])
    in_specs = ([pl.BlockSpec((tm, D_MODEL), tok)] + [full(w) for w in weights]
                + [pl.BlockSpec((tm, HEAD_PAD), tok)] * 3)
    out_shape = (
        jax.ShapeDtypeStruct((n_tok, A_WIDTH), BF16),
        jax.ShapeDtypeStruct((n_tok, QK_PAD), BF16),
        jax.ShapeDtypeStruct((n_tok, QK_PAD), BF16),
        jax.ShapeDtypeStruct((n_tok, B_WIDTH), BF16),
        jax.ShapeDtypeStruct((n_tok, B_WIDTH), F32),
        jax.ShapeDtypeStruct((seq, bsz * C_WIDTH), F32),
        jax.ShapeDtypeStruct((seq, bsz * C_WIDTH), F32),
        jax.ShapeDtypeStruct((n_tok, M_WIDTH), F32),
        jax.ShapeDtypeStruct((n_tok, M_WIDTH), F32),
    )
    out_specs = (
        pl.BlockSpec((tm, A_WIDTH), tok),
        pl.BlockSpec((tm, QK_PAD), tok),
        pl.BlockSpec((tm, QK_PAD), tok),
        pl.BlockSpec((tm, B_WIDTH), tok),
        pl.BlockSpec((tm, B_WIDTH), tok),
        pl.BlockSpec((tm, C_WIDTH), seq_major),
        pl.BlockSpec((tm, C_WIDTH), seq_major),
        pl.BlockSpec((tm, M_WIDTH), tok),
        pl.BlockSpec((tm, M_WIDTH), tok),
    )
    return pl.pallas_call(
        _inproj_body,
        out_shape=out_shape,
        grid=(n_tok // tm,),
        in_specs=in_specs,
        out_specs=out_specs,
        compiler_params=pltpu.CompilerParams(
            dimension_semantics=("parallel",), vmem_limit_bytes=V7X_VMEM_LIMIT_BYTES),
        name="inproj",
    )(x2, *weights, *tabs)


def _attn_body(q_ref, k_ref, v_ref, g_ref, o_ref):
    tq = q_ref.shape[0]
    qi = pl.program_id(1)
    row = lax.broadcasted_iota(jnp.int32, (tq, tq), 0)
    col = lax.broadcasted_iota(jnp.int32, (tq, tq), 1)
    causal = col <= row
    lane = lax.broadcasted_iota(jnp.int32, (tq, HEAD_PAD), 1)

    def head(hd):
        qh = q_ref[:, hd * HEAD_PAD:(hd + 1) * HEAD_PAD]
        pair = slice((hd // 2) * HEAD_PAD, (hd // 2 + 1) * HEAD_PAD)

        def block(j, carry, masked):
            m, l, acc = carry
            r0 = pl.multiple_of(j * tq, tq)
            kj = k_ref[pl.ds(r0, tq), hd * HEAD_PAD:(hd + 1) * HEAD_PAD]
            s = _dot_nt(qh, kj)
            if masked:
                s = jnp.where(causal, s, -jnp.inf)
            m_new = jnp.maximum(m, jnp.max(s, axis=-1, keepdims=True))
            alpha = jnp.exp(m - m_new)
            pr = jnp.exp(s - m_new)
            l = alpha * l + jnp.sum(pr, axis=-1, keepdims=True)
            vj = v_ref[pl.ds(r0, tq), pair]
            acc = alpha * acc + _dot(pr.astype(BF16), vj)
            return m_new, l, acc

        init = (jnp.full((tq, 1), -jnp.inf, F32), jnp.zeros((tq, 1), F32),
                jnp.zeros((tq, HEAD_PAD), F32))
        carry = lax.fori_loop(0, qi, lambda j, c: block(j, c, False), init)
        _, l, acc = block(qi, carry, True)
        return acc / l

    for hp in range(B_HEADS // 2):
        lanes = slice(hp * HEAD_PAD, (hp + 1) * HEAD_PAD)
        o = jnp.where(lane < B_VDIM, head(2 * hp), head(2 * hp + 1))
        o_ref[:, lanes] = (o * g_ref[:, lanes]).astype(BF16)


def _attention(q, k, v, gate, bsz, seq):
    tq = ATTN_TILE
    nq = seq // tq
    return pl.pallas_call(
        _attn_body,
        out_shape=jax.ShapeDtypeStruct((bsz * seq, B_WIDTH), BF16),
        grid=(bsz, nq),
        in_specs=[pl.BlockSpec((tq, QK_PAD), lambda b, i: (b * nq + i, 0)),
                  pl.BlockSpec((seq, QK_PAD), lambda b, i: (b, 0)),
                  pl.BlockSpec((seq, B_WIDTH), lambda b, i: (b, 0)),
                  pl.BlockSpec((tq, B_WIDTH), lambda b, i: (b * nq + i, 0))],
        out_specs=pl.BlockSpec((tq, B_WIDTH), lambda b, i: (b * nq + i, 0)),
        compiler_params=pltpu.CompilerParams(
            dimension_semantics=("parallel", "parallel"),
            vmem_limit_bytes=V7X_VMEM_LIMIT_BYTES),
        name="latent_attention",
    )(q, k, v, gate)


def _s5_body(u_ref, cg_ref, bmat_ref, cmat_ref, are_ref, aim_ref, d_ref, wglu_ref, bglu_ref,
             y_ref, state_ref, buf_ref):
    lt, bsz, width = u_ref.shape
    rows = lt * bsz

    @pl.when(pl.program_id(0) == 0)
    def _():
        state_ref[...] = jnp.zeros_like(state_ref)

    u2 = u_ref[...].reshape(rows, width)
    buf_ref[...] = _dot(u2.astype(BF16), bmat_ref[...])

    for lb in range(C_STATES // S5_LANE_BLOCK):
        re_l = slice(lb * S5_LANE_BLOCK, (lb + 1) * S5_LANE_BLOCK)
        im_l = slice(C_STATES + lb * S5_LANE_BLOCK, C_STATES + (lb + 1) * S5_LANE_BLOCK)
        a_re = jnp.broadcast_to(are_ref[:, re_l], (bsz, S5_LANE_BLOCK))
        a_im = jnp.broadcast_to(aim_ref[:, re_l], (bsz, S5_LANE_BLOCK))

        def step(t, carry):
            s_re, s_im = carry
            r0 = pl.multiple_of(t * bsz, bsz)
            n_re = a_re * s_re - a_im * s_im + buf_ref[pl.ds(r0, bsz), re_l]
            n_im = a_re * s_im + a_im * s_re + buf_ref[pl.ds(r0, bsz), im_l]
            buf_ref[pl.ds(r0, bsz), re_l] = n_re
            buf_ref[pl.ds(r0, bsz), im_l] = n_im
            return n_re, n_im

        s_re, s_im = lax.fori_loop(0, lt, step, (state_ref[:, re_l], state_ref[:, im_l]),
                                   unroll=8)
        state_ref[:, re_l] = s_re
        state_ref[:, im_l] = s_im

    y = _dot(buf_ref[...].astype(BF16), cmat_ref[...]) + d_ref[...] * u2
    y = jax.nn.gelu(y)
    y = y * _sigmoid(_dot(y.astype(BF16), wglu_ref[...]) + bglu_ref[...])
    y_ref[...] = (y * cg_ref[...].reshape(rows, width)).reshape(lt, bsz, width)


def _s5(c_in, c_gate, p, bsz, seq):
    lt = S5_CHUNK
    u3 = c_in.reshape(seq, bsz, C_WIDTH)
    g3 = c_gate.reshape(seq, bsz, C_WIDTH)

    def chunk(i):
        return (i, 0, 0)

    def const(i):
        return (0, 0)

    weights = (p["c_bmat"], p["c_cmat"], p["c_are"], p["c_aim"], p["c_d"], p["c_w_glu"],
               p["c_b_glu"])
    y = pl.pallas_call(
        _s5_body,
        out_shape=jax.ShapeDtypeStruct((seq, bsz, C_WIDTH), F32),
        grid=(seq // lt,),
        in_specs=[pl.BlockSpec((lt, bsz, C_WIDTH), chunk)] * 2
        + [pl.BlockSpec(w.shape, const) for w in weights],
        out_specs=pl.BlockSpec((lt, bsz, C_WIDTH), chunk),
        scratch_shapes=[pltpu.VMEM((bsz, 2 * C_STATES), F32),
                        pltpu.VMEM((lt * bsz, 2 * C_STATES), F32)],
        compiler_params=pltpu.CompilerParams(
            dimension_semantics=("arbitrary",), vmem_limit_bytes=V7X_VMEM_LIMIT_BYTES),
        name="s5_scan",
    )(u3, g3, *weights)
    return y.reshape(seq, bsz * C_WIDTH)


def _head_rms(x, gain, n_heads, head_dim):
    head_id = lax.broadcasted_iota(jnp.int32, x.shape, 1) // head_dim
    out = jnp.zeros_like(x)
    for hd in range(n_heads):
        xh = jnp.where(head_id == hd, x, 0.0)
        ssq = jnp.sum(xh * xh, axis=-1, keepdims=True)
        out = out + xh * lax.rsqrt(ssq * (1.0 / head_dim) + EPS)
    return out * gain


def _memkv_body(mem_ref, ng_ref, wkv_ref, gk_ref, k_ref, v_ref):
    mh = _rms(mem_ref[...], ng_ref[...]).astype(BF16)
    kv = _dot(mh, wkv_ref[...])
    k_ref[...] = _head_rms(kv[:, :M_WIDTH], gk_ref[...], M_HEADS, M_HEAD_DIM).astype(BF16)
    v_ref[...] = kv[:, M_WIDTH:].astype(BF16)


def _memkv(mem2, p, bsz, mem_len):
    def const(b):
        return (0, 0)

    weights = (p["m_norm_g"], p["m_w_kv"], p["m_gk"])
    out = jax.ShapeDtypeStruct((bsz * mem_len, M_WIDTH), BF16)
    return pl.pallas_call(
        _memkv_body,
        out_shape=(out, out),
        grid=(bsz,),
        in_specs=[pl.BlockSpec((mem_len, D_MODEL), lambda b: (b, 0))]
        + [pl.BlockSpec(w.shape, const) for w in weights],
        out_specs=(pl.BlockSpec((mem_len, M_WIDTH), lambda b: (b, 0)),) * 2,
        compiler_params=pltpu.CompilerParams(dimension_semantics=("parallel",)),
        name="memory_kv",
    )(mem2, *weights)


def _merge_body(x_ref, ng_ref, wm_ref, bm_ref, ya_ref, yb_ref, yc_ref, mq_ref, mg_ref,
                mk_ref, mv_ref, gq_ref, wa_ref, wb_ref, wc_ref, wmm_ref, wo_ref, o_ref):
    x = x_ref[...]
    h = _rms(x, ng_ref[...]).astype(BF16)

    mq = mq_ref[...]
    head_id = lax.broadcasted_iota(jnp.int32, mq.shape, 1) // M_HEAD_DIM
    mk = mk_ref[...]
    mv = mv_ref[...]
    om = jnp.zeros_like(mq)
    for hd in range(M_HEADS):
        sel = head_id == hd
        qh = jnp.where(sel, mq, 0.0)
        ssq = jnp.sum(qh * qh, axis=-1, keepdims=True)
        qh = qh * lax.rsqrt(ssq * (1.0 / M_HEAD_DIM) + EPS) * gq_ref[...] * (M_HEAD_DIM ** -0.5)
        s = _dot_nt(qh.astype(BF16), mk)
        e = jnp.exp(s - jnp.max(s, axis=-1, keepdims=True))
        pr = e / jnp.sum(e, axis=-1, keepdims=True)
        om = om + jnp.where(sel, _dot(pr.astype(BF16), mv), 0.0)
    ym = (om * mg_ref[...]).astype(BF16)

    branches = ((ya_ref[...], wa_ref), (yb_ref[...], wb_ref),
                (yc_ref[...].astype(BF16), wc_ref), (ym, wmm_ref))
    merged = jnp.zeros(x.shape, F32)
    for br, (y, w_ref) in enumerate(branches):
        cols = slice(br * D_MODEL, (br + 1) * D_MODEL)
        gate = _sigmoid(_dot(h, wm_ref[:, cols]) + bm_ref[:, cols])
        merged = merged + gate * _dot(y, w_ref[...])
    o_ref[...] = x + _dot(merged.astype(BF16), wo_ref[...])


def _merge(x2, ya, yb, yc, mq, mg, mk, mv, p, bsz, seq, mem_len):
    n_tok = x2.shape[0]
    tm = TOKEN_TILE
    nst = seq // tm

    def tok(i):
        return (i, 0)

    def seq_major(i):
        return (i % nst, i // nst)

    def per_batch(i):
        return (i // nst, 0)

    def const(i):
        return (0, 0)

    def resident(a):
        return pl.BlockSpec(a.shape, const, pipeline_mode=pl.Buffered(1))

    in_specs = [
        pl.BlockSpec((tm, D_MODEL), tok),
        resident(p["norm_g"]), resident(p["w_merge"]), resident(p["b_merge"]),
        pl.BlockSpec((tm, A_WIDTH), tok),
        pl.BlockSpec((tm, B_WIDTH), tok),
        pl.BlockSpec((tm, C_WIDTH), seq_major),
        pl.BlockSpec((tm, M_WIDTH), tok),
        pl.BlockSpec((tm, M_WIDTH), tok),
        pl.BlockSpec((mem_len, M_WIDTH), per_batch),
        pl.BlockSpec((mem_len, M_WIDTH), per_batch),
        resident(p["m_gq"]), resident(p["w_br_a"]), resident(p["w_br_b"]),
        resident(p["w_br_c"]), resident(p["w_br_m"]), resident(p["w_out"]),
    ]
    return pl.pallas_call(
        _merge_body,
        out_shape=jax.ShapeDtypeStruct((n_tok, D_MODEL), F32),
        grid=(n_tok // tm,),
        in_specs=in_specs,
        out_specs=pl.BlockSpec((tm, D_MODEL), tok),
        compiler_params=pltpu.CompilerParams(
            dimension_semantics=("parallel",), vmem_limit_bytes=V7X_VMEM_LIMIT_BYTES),
        name="merge",
    )(x2, p["norm_g"], p["w_merge"], p["b_merge"], ya, yb, yc, mq, mg, mk, mv,
      p["m_gq"], p["w_br_a"], p["w_br_b"], p["w_br_c"], p["w_br_m"], p["w_out"])


def _row(v):
    return v.reshape(1, -1).astype(F32)


def _prep_layer(norm_g, w_in, b_merge, a_norm_g, a_w_s, a_b_s, b_q_norm_g, b_kv_norm_g,
                b_w_uq, b_w_ukv, b_qk_g_q, b_qk_g_k, c_a_re, c_a_im, c_log_dt, c_b_re,
                c_b_im, c_c_re, c_c_im, c_d, c_w_glu, c_b_glu, m_norm_g, m_w_kv, m_qk_g_q,
                m_qk_g_k, w_br_a, w_br_b, w_br_c, w_br_m, w_out):
    p = {}
    p["norm_g"] = _row(norm_g)
    kpe = jnp.zeros((D_MODEL, HEAD_PAD), F32).at[:, B_NOPE:B_QK_DIM].set(w_in[:, 1792:1824])
    p["w_small"] = jnp.concatenate([w_in[:, :1792], kpe, w_in[:, 1824:3360]], axis=1).astype(BF16)
    p["w_merge"] = w_in[:, 3360:].astype(BF16)
    p["b_merge"] = _row(b_merge)

    p["a_norm_g"] = _row(a_norm_g)
    causal = jnp.tril(jnp.ones((A_CHUNK, A_CHUNK), dtype=bool))
    p["a_wst"] = jnp.where(causal, a_w_s, 0.0).reshape(A_GROUPS * A_CHUNK, A_CHUNK).astype(BF16)
    p["a_bias"] = jnp.repeat(a_b_s.T, A_GROUP_DIM, axis=1).astype(F32)

    p["b_q_norm_g"] = _row(b_q_norm_g)
    p["b_kv_norm_g"] = _row(b_kv_norm_g)
    wuq = b_w_uq.reshape(B_Q_RANK, B_HEADS, B_QK_DIM)
    wuq = jnp.pad(wuq, ((0, 0), (0, 0), (0, HEAD_PAD - B_QK_DIM)))
    p["b_wuq"] = wuq.reshape(B_Q_RANK, QK_PAD).astype(BF16)
    wukv = b_w_ukv.reshape(B_KV_RANK, B_HEADS, B_NOPE + B_VDIM)
    wuk = jnp.pad(wukv[:, :, :B_NOPE], ((0, 0), (0, 0), (0, HEAD_PAD - B_NOPE)))
    p["b_wuk"] = wuk.reshape(B_KV_RANK, QK_PAD).astype(BF16)
    p["b_wuv"] = wukv[:, :, B_NOPE:].reshape(B_KV_RANK, B_WIDTH).astype(BF16)
    p["b_gq"] = _row(jnp.pad(b_qk_g_q, (0, HEAD_PAD - B_QK_DIM)))
    p["b_gk"] = _row(jnp.pad(b_qk_g_k, (0, HEAD_PAD - B_QK_DIM)))

    lam = lax.complex(c_a_re.astype(F32), c_a_im.astype(F32))
    dt = jnp.exp(c_log_dt.astype(F32))[:, None]
    a_bar = jnp.exp(lam * dt)
    b_mat = lax.complex(c_b_re.astype(F32), c_b_im.astype(F32))
    b_bar = ((a_bar - 1.0) / lam)[..., None] * b_mat
    eye = jnp.eye(C_GROUPS, dtype=F32)

    def expand_in(m):
        return (eye[:, None, :, None] * m.transpose(0, 2, 1)[:, :, None, :]).reshape(
            C_WIDTH, C_STATES)

    def expand_out(m):
        return (eye[:, None, :, None] * m.transpose(0, 2, 1)[:, :, None, :]).reshape(
            C_STATES, C_WIDTH)

    p["c_bmat"] = jnp.concatenate([expand_in(jnp.real(b_bar)), expand_in(jnp.imag(b_bar))],
                                  axis=1).astype(BF16)
    p["c_cmat"] = jnp.concatenate([expand_out(c_c_re.astype(F32)),
                                   -expand_out(c_c_im.astype(F32))], axis=0).astype(BF16)
    p["c_are"] = _row(jnp.real(a_bar))
    p["c_aim"] = _row(jnp.imag(a_bar))
    p["c_d"] = _row(c_d)
    p["c_w_glu"] = c_w_glu.astype(BF16)
    p["c_b_glu"] = _row(c_b_glu)

    p["m_norm_g"] = _row(m_norm_g)
    p["m_w_kv"] = m_w_kv.astype(BF16)
    p["m_gq"] = _row(jnp.tile(m_qk_g_q, M_HEADS))
    p["m_gk"] = _row(jnp.tile(m_qk_g_k, M_HEADS))
    p["w_br_a"] = w_br_a.astype(BF16)
    p["w_br_b"] = w_br_b.astype(BF16)
    p["w_br_c"] = w_br_c.astype(BF16)
    p["w_br_m"] = w_br_m.astype(BF16)
    p["w_out"] = w_out.astype(BF16)
    return p


def kernel(x, mem, positions, norm_g, w_in, b_merge, a_norm_g, a_w_s, a_b_s, b_q_norm_g, b_kv_norm_g, b_w_uq, b_w_ukv, b_qk_g_q, b_qk_g_k, c_a_re, c_a_im, c_log_dt, c_b_re, c_b_im, c_c_re, c_c_im, c_d, c_w_glu, c_b_glu, m_norm_g, m_w_kv, m_qk_g_q, m_qk_g_k, w_br_a, w_br_b, w_br_c, w_br_m, w_out):
    bsz, seq, d_model = x.shape
    mem_len = mem.shape[1]
    depth = norm_g.shape[0]
    assert d_model == D_MODEL and seq % TOKEN_TILE == 0 and seq % ATTN_TILE == 0
    assert seq % S5_CHUNK == 0 and bsz == 8

    stacked = (norm_g, w_in, b_merge, a_norm_g, a_w_s, a_b_s, b_q_norm_g, b_kv_norm_g,
               b_w_uq, b_w_ukv, b_qk_g_q, b_qk_g_k, c_a_re, c_a_im, c_log_dt, c_b_re,
               c_b_im, c_c_re, c_c_im, c_d, c_w_glu, c_b_glu, m_norm_g, m_w_kv, m_qk_g_q,
               m_qk_g_k, w_br_a, w_br_b, w_br_c, w_br_m, w_out)

    tabs = _rope_tables(positions)
    x2 = x.reshape(bsz * seq, d_model)
    mem2 = mem.reshape(bsz * mem_len, d_model)
    for layer in range(depth):
        p = _prep_layer(*(t[layer] for t in stacked))
        ya, q, k, v, bg, ci, cg, mq, mg = _inproj(x2, p, tabs, bsz, seq)
        yb = _attention(q, k, v, bg, bsz, seq)
        yc = _s5(ci, cg, p, bsz, seq)
        mk, mv = _memkv(mem2, p, bsz, mem_len)
        x2 = _merge(x2, ya, yb, yc, mq, mg, mk, mv, p, bsz, seq, mem_len)
    return x2.reshape(bsz, seq, d_model)
```

```python
import functools
import math

import jax
import jax.numpy as jnp
from jax import lax
from jax.experimental import pallas as pl
from jax.experimental.pallas import tpu as pltpu

F32 = jnp.float32
BF16 = jnp.bfloat16

D_MODEL = 1024
EPS = 1e-6
N_BRANCH = 4

A_WIDTH = 256
A_GROUPS = 4
A_GROUP_DIM = A_WIDTH // A_GROUPS
A_CHUNK = 128

B_HEADS = 8
B_NOPE = 64
B_ROPE = 32
B_QK_DIM = B_NOPE + B_ROPE
B_VDIM = 64
B_Q_RANK = 768
B_KV_RANK = 256
B_WIDTH = B_HEADS * B_VDIM
ROPE_THETA = 10000.0
HEAD_PAD = 128
QK_PAD = B_HEADS * HEAD_PAD

C_WIDTH = 256
C_GROUP = 16
C_GROUPS = C_WIDTH // C_GROUP
C_STATE = 64
C_STATES = C_GROUPS * C_STATE

M_HEADS = 4
M_HEAD_DIM = 64
M_WIDTH = M_HEADS * M_HEAD_DIM

COL_AU, COL_AV, COL_AG = 0, 256, 512
COL_CQ = 768
COL_CKV = 1536
COL_KPE = 1792
COL_BG = 1920
COL_CI = 2432
COL_CG = 2688
COL_MQ = 2944
COL_MG = 3200
N_SMALL = 3456

V7X_VMEM_LIMIT_BYTES = 56 * 1024 * 1024

TOKEN_TILE = 256
ATTN_TILE = 256
S5_CHUNK = 64
S5_LANE_BLOCK = 512


def _sigmoid(x):
    return 1.0 / (1.0 + jnp.exp(-x))


def _silu(x):
    return x * _sigmoid(x)


def _rms(x, g):
    return x * lax.rsqrt(jnp.mean(x * x, axis=-1, keepdims=True) + EPS) * g


def _dot(a, b):
    return jnp.dot(a, b, preferred_element_type=F32)


def _dot_nt(a, b):
    return lax.dot_general(a, b, (((1,), (1,)), ((), ())), preferred_element_type=F32)


def _rope_table_body(pos_ref, freq_ref, c_ref, sa_ref, sb_ref):
    ang = pos_ref[...] * freq_ref[...]
    lane = lax.broadcasted_iota(jnp.int32, ang.shape, 1)
    first = (lane >= B_NOPE) & (lane < B_NOPE + B_ROPE // 2)
    second = (lane >= B_NOPE + B_ROPE // 2) & (lane < B_QK_DIM)
    cos = jnp.cos(ang)
    sin = jnp.sin(ang)
    c_ref[...] = jnp.where(first | second, cos, 1.0)
    sa_ref[...] = jnp.where(second, sin, 0.0)
    sb_ref[...] = jnp.where(first, -sin, 0.0)


def _rope_tables(positions):
    n_tok = positions.size
    half = B_ROPE // 2
    inv_freq = ROPE_THETA ** (-jnp.arange(half, dtype=F32) / half)
    freq_row = jnp.zeros((1, HEAD_PAD), F32)
    freq_row = freq_row.at[0, B_NOPE:B_NOPE + half].set(inv_freq)
    freq_row = freq_row.at[0, B_NOPE + half:B_QK_DIM].set(inv_freq)
    pos = positions.astype(F32).reshape(n_tok, 1)
    tm = 1024
    tab = jax.ShapeDtypeStruct((n_tok, HEAD_PAD), F32)
    return pl.pallas_call(
        _rope_table_body,
        out_shape=(tab, tab, tab),
        grid=(n_tok // tm,),
        in_specs=[pl.BlockSpec((tm, 1), lambda i: (i, 0)),
                  pl.BlockSpec((1, HEAD_PAD), lambda i: (0, 0))],
        out_specs=tuple(pl.BlockSpec((tm, HEAD_PAD), lambda i: (i, 0)) for _ in range(3)),
        compiler_params=pltpu.CompilerParams(dimension_semantics=("parallel",)),
        name="rope_tables",
    )(pos, freq_row)


def _rope(x, c, sa, sb):
    return x * c + pltpu.roll(x, 16, 1) * sa + pltpu.roll(x, HEAD_PAD - 16, 1) * sb


def _inproj_body(x_ref, ng_ref, w_ref, ang_ref, wst_ref, bias_ref, qng_ref, kvng_ref,
                 wuq_ref, wuk_ref, wuvt_ref, gq_ref, gk_ref, c_ref, sa_ref, sb_ref,
                 ya_ref, q_ref, k_ref, vt_ref, bg_ref, ci_ref, cg_ref, mq_ref, mg_ref):
    tm = x_ref.shape[0]
    h = _rms(x_ref[...], ng_ref[...]).astype(BF16)

    def proj(lo, hi):
        return _dot(h, w_ref[:, lo:hi])

    bg_ref[...] = _silu(proj(COL_BG, COL_CI))
    ci_ref[...] = proj(COL_CI, COL_CG)
    cg_ref[...] = _silu(proj(COL_CG, COL_MQ))
    mq_ref[...] = proj(COL_MQ, COL_MG)
    mg_ref[...] = _silu(proj(COL_MG, N_SMALL))

    u = jax.nn.gelu(proj(COL_AU, COL_AV))
    v = _rms(jax.nn.gelu(proj(COL_AV, COL_AG)), ang_ref[...]).astype(BF16)
    ag = _silu(proj(COL_AG, COL_CQ))
    lane_group = lax.broadcasted_iota(jnp.int32, (A_CHUNK, A_WIDTH), 1) // A_GROUP_DIM
    for c in range(tm // A_CHUNK):
        rows = slice(c * A_CHUNK, (c + 1) * A_CHUNK)
        s_all = _dot(wst_ref[...], v[rows, :])
        s = s_all[(A_GROUPS - 1) * A_CHUNK:, :]
        for g in range(A_GROUPS - 2, -1, -1):
            s = jnp.where(lane_group == g, s_all[g * A_CHUNK:(g + 1) * A_CHUNK, :], s)
        s = s + bias_ref[...]
        ya_ref[rows, :] = (u[rows, :] * s * ag[rows, :]).astype(BF16)

    c_tab, sa_tab, sb_tab = c_ref[...], sa_ref[...], sb_ref[...]
    q_scale = B_QK_DIM ** -0.5 * math.log2(math.e)
    cq =_rms(proj(COL_CQ, COL_CKV), qng_ref[...]).astype(BF16)
    qf = _dot(cq, wuq_ref[...])
    for hd in range(B_HEADS):
        lanes = slice(hd * HEAD_PAD, (hd + 1) * HEAD_PAD)
        qh = qf[:, lanes]
        ssq = jnp.sum(qh * qh, axis=-1, keepdims=True)
        qh = qh * lax.rsqrt(ssq * (1.0 / B_QK_DIM) + EPS) * gq_ref[...]
        q_ref[:, lanes] = (_rope(qh, c_tab, sa_tab, sb_tab) * q_scale).astype(BF16)

    ckv = _rms(proj(COL_CKV, COL_KPE), kvng_ref[...]).astype(BF16)
    vt_ref[...] = _dot_nt(wuvt_ref[...], ckv).astype(BF16)
    kf = _dot(ckv, wuk_ref[...])
    kpe = proj(COL_KPE, COL_BG)
    for hd in range(B_HEADS):
        lanes = slice(hd * HEAD_PAD, (hd + 1) * HEAD_PAD)
        kh = kf[:, lanes] + kpe
        ssq = jnp.sum(kh * kh, axis=-1, keepdims=True)
        kh = kh * lax.rsqrt(ssq * (1.0 / B_QK_DIM) + EPS) * gk_ref[...]
        k_ref[:, lanes] = _rope(kh, c_tab, sa_tab, sb_tab).astype(BF16)


def _inproj(x2, p, tabs, bsz, seq):
    n_tok = x2.shape[0]
    tm = TOKEN_TILE
    nst = seq // tm

    def tok(i):
        return (i, 0)

    def seq_major(i):
        return (i % nst, i // nst)

    def const(i):
        return (0, 0)

    def full(a):
        return pl.BlockSpec(a.shape, const)

    weights = (p["norm_g"], p["w_small"], p["a_norm_g"], p["a_wst"], p["a_bias"],
               p["b_q_norm_g"], p["b_kv_norm_g"], p["b_wuq"], p["b_wuk"], p["b_wuvt"],
               p["b_gq"], p["b_gk"])
    in_specs = ([pl.BlockSpec((tm, D_MODEL), tok)] + [full(w) for w in weights]
                + [pl.BlockSpec((tm, HEAD_PAD), tok)] * 3)
    out_shape = (
        jax.ShapeDtypeStruct((n_tok, A_WIDTH), BF16),
        jax.ShapeDtypeStruct((n_tok, QK_PAD), BF16),
        jax.ShapeDtypeStruct((n_tok, QK_PAD), BF16),
        jax.ShapeDtypeStruct((n_tok // tm, B_WIDTH, tm), BF16),
        jax.ShapeDtypeStruct((n_tok, B_WIDTH), F32),
        jax.ShapeDtypeStruct((seq, bsz * C_WIDTH), F32),
        jax.ShapeDtypeStruct((seq, bsz * C_WIDTH), F32),
        jax.ShapeDtypeStruct((n_tok, M_WIDTH), F32),
        jax.ShapeDtypeStruct((n_tok, M_WIDTH), F32),
    )
    out_specs = (
        pl.BlockSpec((tm, A_WIDTH), tok),
        pl.BlockSpec((tm, QK_PAD), tok),
        pl.BlockSpec((tm, QK_PAD), tok),
        pl.BlockSpec((None, B_WIDTH, tm), lambda i: (i, 0, 0)),
        pl.BlockSpec((tm, B_WIDTH), tok),
        pl.BlockSpec((tm, C_WIDTH), seq_major),
        pl.BlockSpec((tm, C_WIDTH), seq_major),
        pl.BlockSpec((tm, M_WIDTH), tok),
        pl.BlockSpec((tm, M_WIDTH), tok),
    )
    return pl.pallas_call(
        _inproj_body,
        out_shape=out_shape,
        grid=(n_tok // tm,),
        in_specs=in_specs,
        out_specs=out_specs,
        compiler_params=pltpu.CompilerParams(
            dimension_semantics=("parallel",), vmem_limit_bytes=V7X_VMEM_LIMIT_BYTES),
        name="inproj",
    )(x2, *weights, *tabs)


def _attn_body(q_ref, k_ref, vt_ref, g_ref, o_ref, m_ref, l_ref, acc_ref, s_ref):
    tq = q_ref.shape[0]
    qi = pl.program_id(1)
    key_le_query = (lax.broadcasted_iota(jnp.int32, (tq, tq), 0)
                    <= lax.broadcasted_iota(jnp.int32, (tq, tq), 1))
    m_ref[...] = jnp.full(m_ref.shape, -jnp.inf, F32)
    l_ref[...] = jnp.zeros(l_ref.shape, F32)
    acc_ref[...] = jnp.zeros(acc_ref.shape, F32)

    def block(j, masked):
        r0 = pl.multiple_of(j * tq, tq)
        alphas = []
        for hd in range(B_HEADS):
            lanes = slice(hd * HEAD_PAD, (hd + 1) * HEAD_PAD)
            st = _dot_nt(k_ref[pl.ds(r0, tq), lanes], q_ref[:, lanes])
            if masked:
                st = jnp.where(key_le_query, st, -jnp.inf)
            s_ref[hd] = st
            m_old = m_ref[hd:hd + 1, :]
            m_new = jnp.maximum(m_old, jnp.max(st, axis=0, keepdims=True))
            m_ref[hd:hd + 1, :] = m_new
            alphas.append(jnp.exp2(m_old - m_new))
        for hd in range(B_HEADS):
            rows = slice(hd * B_VDIM, (hd + 1) * B_VDIM)
            pt = jnp.exp2(s_ref[hd] - m_ref[hd:hd + 1, :])
            l_ref[hd:hd + 1, :] = (alphas[hd] * l_ref[hd:hd + 1, :]
                                   + jnp.sum(pt, axis=0, keepdims=True))
            acc_ref[rows, :] = (alphas[hd] * acc_ref[rows, :]
                                + _dot(vt_ref[j, rows, :], pt.astype(BF16)))

    def full_block(j, carry):
        block(j, False)
        return carry

    lax.fori_loop(0, qi, full_block, 0)
    block(qi, True)

    for hd in range(B_HEADS):
        rows = slice(hd * B_VDIM, (hd + 1) * B_VDIM)
        acc_ref[rows, :] = acc_ref[rows, :] * (1.0 / l_ref[hd:hd + 1, :])
    o_ref[...] = (acc_ref[...].T * g_ref[...]).astype(BF16)


def _attention(q, k, vt, gate, bsz, seq):
    tq = ATTN_TILE
    nq = seq // tq
    return pl.pallas_call(
        _attn_body,
        out_shape=jax.ShapeDtypeStruct((bsz * seq, B_WIDTH), BF16),
        grid=(bsz, nq),
        in_specs=[pl.BlockSpec((tq, QK_PAD), lambda b, i: (b * nq + i, 0)),
                  pl.BlockSpec((seq, QK_PAD), lambda b, i: (b, 0)),
                  pl.BlockSpec((nq, B_WIDTH, tq), lambda b, i: (b, 0, 0)),
                  pl.BlockSpec((tq, B_WIDTH), lambda b, i: (b * nq + i, 0))],
        out_specs=pl.BlockSpec((tq, B_WIDTH), lambda b, i: (b * nq + i, 0)),
        scratch_shapes=[pltpu.VMEM((B_HEADS, tq), F32), pltpu.VMEM((B_HEADS, tq), F32),
                        pltpu.VMEM((B_WIDTH, tq), F32), pltpu.VMEM((B_HEADS, tq, tq), F32)],
        compiler_params=pltpu.CompilerParams(
            dimension_semantics=("parallel", "parallel"),
            vmem_limit_bytes=V7X_VMEM_LIMIT_BYTES),
        name="latent_attention",
    )(q, k, vt, gate)


def _s5_discretise_body(are_ref, aim_ref, ldt_ref, bre_ref, bim_ref,
                        abar_re_ref, abar_im_ref, bbar_re_ref, bbar_im_ref):
    a_re, a_im = are_ref[...], aim_ref[...]
    dt = jnp.exp(ldt_ref[...])
    mag = jnp.exp(a_re * dt)
    abar_re = mag * jnp.cos(a_im * dt)
    abar_im = mag * jnp.sin(a_im * dt)
    num_re = abar_re - 1.0
    inv_den = 1.0 / (a_re * a_re + a_im * a_im)
    q_re = (num_re * a_re + abar_im * a_im) * inv_den
    q_im = (abar_im * a_re - num_re * a_im) * inv_den
    b_re, b_im = bre_ref[...], bim_ref[...]
    abar_re_ref[...] = abar_re
    abar_im_ref[...] = abar_im
    bbar_re_ref[...] = q_re * b_re - q_im * b_im
    bbar_im_ref[...] = q_re * b_im + q_im * b_re


def _s5_discretise(a_re, a_im, log_dt, b_re, b_im):
    out = jax.ShapeDtypeStruct(a_re.shape, F32)
    return pl.pallas_call(_s5_discretise_body, out_shape=(out,) * 4, name="s5_discretise")(
        a_re, a_im, log_dt, b_re, b_im)


def _s5_body(u_ref, cg_ref, bmat_ref, cmat_ref, are_ref, aim_ref, d_ref, wglu_ref, bglu_ref,
             y_ref, state_ref, buf_ref):
    lt, bsz, width = u_ref.shape
    rows = lt * bsz

    @pl.when(pl.program_id(0) == 0)
    def _():
        state_ref[...] = jnp.zeros_like(state_ref)

    u2 = u_ref[...].reshape(rows, width)
    buf_ref[...] = _dot(u2.astype(BF16), bmat_ref[...])

    for lb in range(C_STATES // S5_LANE_BLOCK):
        re_l = slice(lb * S5_LANE_BLOCK, (lb + 1) * S5_LANE_BLOCK)
        im_l = slice(C_STATES + lb * S5_LANE_BLOCK, C_STATES + (lb + 1) * S5_LANE_BLOCK)
        a_re = jnp.broadcast_to(are_ref[:, re_l], (bsz, S5_LANE_BLOCK))
        a_im = jnp.broadcast_to(aim_ref[:, re_l], (bsz, S5_LANE_BLOCK))

        def step(t, carry):
            s_re, s_im = carry
            r0 = pl.multiple_of(t * bsz, bsz)
            n_re = a_re * s_re - a_im * s_im + buf_ref[pl.ds(r0, bsz), re_l]
            n_im = a_re * s_im + a_im * s_re + buf_ref[pl.ds(r0, bsz), im_l]
            buf_ref[pl.ds(r0, bsz), re_l] = n_re
            buf_ref[pl.ds(r0, bsz), im_l] = n_im
            return n_re, n_im

        s_re, s_im = lax.fori_loop(0, lt, step, (state_ref[:, re_l], state_ref[:, im_l]),
                                   unroll=8)
        state_ref[:, re_l] = s_re
        state_ref[:, im_l] = s_im

    y = _dot(buf_ref[...].astype(BF16), cmat_ref[...]) + d_ref[...] * u2
    y = jax.nn.gelu(y)
    y = y * _sigmoid(_dot(y.astype(BF16), wglu_ref[...]) + bglu_ref[...])
    y_ref[...] = (y * cg_ref[...].reshape(rows, width)).reshape(lt, bsz, width)


def _s5(c_in, c_gate, p, bsz, seq):
    lt = S5_CHUNK
    u3 = c_in.reshape(seq, bsz, C_WIDTH)
    g3 = c_gate.reshape(seq, bsz, C_WIDTH)

    def chunk(i):
        return (i, 0, 0)

    def const(i):
        return (0, 0)

    weights = (p["c_bmat"], p["c_cmat"], p["c_are"], p["c_aim"], p["c_d"], p["c_w_glu"],
               p["c_b_glu"])
    y = pl.pallas_call(
        _s5_body,
        out_shape=jax.ShapeDtypeStruct((seq, bsz, C_WIDTH), F32),
        grid=(seq // lt,),
        in_specs=[pl.BlockSpec((lt, bsz, C_WIDTH), chunk)] * 2
        + [pl.BlockSpec(w.shape, const) for w in weights],
        out_specs=pl.BlockSpec((lt, bsz, C_WIDTH), chunk),
        scratch_shapes=[pltpu.VMEM((bsz, 2 * C_STATES), F32),
                        pltpu.VMEM((lt * bsz, 2 * C_STATES), F32)],
        compiler_params=pltpu.CompilerParams(
            dimension_semantics=("arbitrary",), vmem_limit_bytes=V7X_VMEM_LIMIT_BYTES),
        name="s5_scan",
    )(u3, g3, *weights)
    return y.reshape(seq, bsz * C_WIDTH)


def _head_rms(x, gain, n_heads, head_dim):
    head_id = lax.broadcasted_iota(jnp.int32, x.shape, 1) // head_dim
    out = jnp.zeros_like(x)
    for hd in range(n_heads):
        xh = jnp.where(head_id == hd, x, 0.0)
        ssq = jnp.sum(xh * xh, axis=-1, keepdims=True)
        out = out + xh * lax.rsqrt(ssq * (1.0 / head_dim) + EPS)
    return out * gain


def _memkv_body(mem_ref, ng_ref, wkv_ref, gk_ref, k_ref, v_ref):
    mh = _rms(mem_ref[...], ng_ref[...]).astype(BF16)
    kv = _dot(mh, wkv_ref[...])
    k_ref[...] = _head_rms(kv[:, :M_WIDTH], gk_ref[...], M_HEADS, M_HEAD_DIM).astype(BF16)
    v_ref[...] = kv[:, M_WIDTH:].astype(BF16)


def _memkv(mem2, p, bsz, mem_len):
    def const(b):
        return (0, 0)

    weights = (p["m_norm_g"], p["m_w_kv"], p["m_gk"])
    out = jax.ShapeDtypeStruct((bsz * mem_len, M_WIDTH), BF16)
    return pl.pallas_call(
        _memkv_body,
        out_shape=(out, out),
        grid=(bsz,),
        in_specs=[pl.BlockSpec((mem_len, D_MODEL), lambda b: (b, 0))]
        + [pl.BlockSpec(w.shape, const) for w in weights],
        out_specs=(pl.BlockSpec((mem_len, M_WIDTH), lambda b: (b, 0)),) * 2,
        compiler_params=pltpu.CompilerParams(dimension_semantics=("parallel",)),
        name="memory_kv",
    )(mem2, *weights)


def _merge_body(x_ref, ng_ref, wm_ref, bm_ref, ya_ref, yb_ref, yc_ref, mq_ref, mg_ref,
                mk_ref, mv_ref, gq_ref, wa_ref, wb_ref, wc_ref, wmm_ref, wo_ref, o_ref):
    x = x_ref[...]
    h = _rms(x, ng_ref[...]).astype(BF16)

    mq = mq_ref[...]
    head_id = lax.broadcasted_iota(jnp.int32, mq.shape, 1) // M_HEAD_DIM
    mk = mk_ref[...]
    mv = mv_ref[...]
    om = jnp.zeros_like(mq)
    for hd in range(M_HEADS):
        sel = head_id == hd
        qh = jnp.where(sel, mq, 0.0)
        ssq = jnp.sum(qh * qh, axis=-1, keepdims=True)
        qh = qh * lax.rsqrt(ssq * (1.0 / M_HEAD_DIM) + EPS) * gq_ref[...] * (M_HEAD_DIM ** -0.5)
        s = _dot_nt(qh.astype(BF16), mk)
        e = jnp.exp(s - jnp.max(s, axis=-1, keepdims=True))
        pr = e / jnp.sum(e, axis=-1, keepdims=True)
        om = om + jnp.where(sel, _dot(pr.astype(BF16), mv), 0.0)
    ym = (om * mg_ref[...]).astype(BF16)

    branches = ((ya_ref[...], wa_ref), (yb_ref[...], wb_ref),
                (yc_ref[...].astype(BF16), wc_ref), (ym, wmm_ref))
    merged = jnp.zeros(x.shape, F32)
    for br, (y, w_ref) in enumerate(branches):
        cols = slice(br * D_MODEL, (br + 1) * D_MODEL)
        gate = _sigmoid(_dot(h, wm_ref[:, cols]) + bm_ref[:, cols])
        merged = merged + gate * _dot(y, w_ref[...])
    o_ref[...] = x + _dot(merged.astype(BF16), wo_ref[...])


def _merge(x2, ya, yb, yc, mq, mg, mk, mv, p, bsz, seq, mem_len):
    n_tok = x2.shape[0]
    tm = TOKEN_TILE
    nst = seq // tm

    def tok(i):
        return (i, 0)

    def seq_major(i):
        return (i % nst, i // nst)

    def per_batch(i):
        return (i // nst, 0)

    def const(i):
        return (0, 0)

    def resident(a):
        return pl.BlockSpec(a.shape, const, pipeline_mode=pl.Buffered(1))

    in_specs = [
        pl.BlockSpec((tm, D_MODEL), tok),
        resident(p["norm_g"]), resident(p["w_merge"]), resident(p["b_merge"]),
        pl.BlockSpec((tm, A_WIDTH), tok),
        pl.BlockSpec((tm, B_WIDTH), tok),
        pl.BlockSpec((tm, C_WIDTH), seq_major),
        pl.BlockSpec((tm, M_WIDTH), tok),
        pl.BlockSpec((tm, M_WIDTH), tok),
        pl.BlockSpec((mem_len, M_WIDTH), per_batch),
        pl.BlockSpec((mem_len, M_WIDTH), per_batch),
        resident(p["m_gq"]), resident(p["w_br_a"]), resident(p["w_br_b"]),
        resident(p["w_br_c"]), resident(p["w_br_m"]), resident(p["w_out"]),
    ]
    return pl.pallas_call(
        _merge_body,
        out_shape=jax.ShapeDtypeStruct((n_tok, D_MODEL), F32),
        grid=(n_tok // tm,),
        in_specs=in_specs,
        out_specs=pl.BlockSpec((tm, D_MODEL), tok),
        compiler_params=pltpu.CompilerParams(
            dimension_semantics=("parallel",), vmem_limit_bytes=V7X_VMEM_LIMIT_BYTES),
        name="merge",
    )(x2, p["norm_g"], p["w_merge"], p["b_merge"], ya, yb, yc, mq, mg, mk, mv,
      p["m_gq"], p["w_br_a"], p["w_br_b"], p["w_br_c"], p["w_br_m"], p["w_out"])


def _row(v):
    return v.reshape(1, -1).astype(F32)


def _prep_layer(norm_g, w_in, b_merge, a_norm_g, a_w_s, a_b_s, b_q_norm_g, b_kv_norm_g,
                b_w_uq, b_w_ukv, b_qk_g_q, b_qk_g_k, c_a_re, c_a_im, c_log_dt, c_b_re,
                c_b_im, c_c_re, c_c_im, c_d, c_w_glu, c_b_glu, m_norm_g, m_w_kv, m_qk_g_q,
                m_qk_g_k, w_br_a, w_br_b, w_br_c, w_br_m, w_out):
    p = {}
    p["norm_g"] = _row(norm_g)
    kpe = jnp.zeros((D_MODEL, HEAD_PAD), F32).at[:, B_NOPE:B_QK_DIM].set(w_in[:, 1792:1824])
    p["w_small"] = jnp.concatenate([w_in[:, :1792], kpe, w_in[:, 1824:3360]], axis=1).astype(BF16)
    p["w_merge"] = w_in[:, 3360:].astype(BF16)
    p["b_merge"] = _row(b_merge)

    p["a_norm_g"] = _row(a_norm_g)
    causal = jnp.tril(jnp.ones((A_CHUNK, A_CHUNK), dtype=bool))
    p["a_wst"] = jnp.where(causal, a_w_s, 0.0).reshape(A_GROUPS * A_CHUNK, A_CHUNK).astype(BF16)
    p["a_bias"] = jnp.repeat(a_b_s.T, A_GROUP_DIM, axis=1).astype(F32)

    p["b_q_norm_g"] = _row(b_q_norm_g)
    p["b_kv_norm_g"] = _row(b_kv_norm_g)
    wuq = b_w_uq.reshape(B_Q_RANK, B_HEADS, B_QK_DIM)
    wuq = jnp.pad(wuq, ((0, 0), (0, 0), (0, HEAD_PAD - B_QK_DIM)))
    p["b_wuq"] = wuq.reshape(B_Q_RANK, QK_PAD).astype(BF16)
    wukv = b_w_ukv.reshape(B_KV_RANK, B_HEADS, B_NOPE + B_VDIM)
    wuk = jnp.pad(wukv[:, :, :B_NOPE], ((0, 0), (0, 0), (0, HEAD_PAD - B_NOPE)))
    p["b_wuk"] = wuk.reshape(B_KV_RANK, QK_PAD).astype(BF16)
    p["b_wuvt"] = wukv[:, :, B_NOPE:].reshape(B_KV_RANK, B_WIDTH).T.astype(BF16)
    p["b_gq"] = _row(jnp.pad(b_qk_g_q, (0, HEAD_PAD - B_QK_DIM)))
    p["b_gk"] = _row(jnp.pad(b_qk_g_k, (0, HEAD_PAD - B_QK_DIM)))

    def rep(a):
        return jnp.repeat(a.astype(F32), C_GROUP, axis=0)

    def gcp(b):
        return b.astype(F32).transpose(0, 2, 1).reshape(C_WIDTH, C_STATE)

    log_dt = jnp.broadcast_to(c_log_dt.astype(F32)[:, None], (C_GROUPS, C_STATE))
    abar_re, abar_im, bbar_re, bbar_im = _s5_discretise(
        rep(c_a_re), rep(c_a_im), rep(log_dt), gcp(c_b_re), gcp(c_b_im))
    eye = jnp.eye(C_GROUPS, dtype=F32)

    def expand_in(m):
        m = m.reshape(C_GROUPS, C_GROUP, C_STATE)
        return (eye[:, None, :, None] * m[:, :, None, :]).reshape(C_WIDTH, C_STATES)

    def expand_out(m):
        return (eye[:, None, :, None] * m.transpose(0, 2, 1)[:, :, None, :]).reshape(
            C_STATES, C_WIDTH)

    p["c_bmat"] = jnp.concatenate([expand_in(bbar_re), expand_in(bbar_im)], axis=1).astype(BF16)
    p["c_cmat"] = jnp.concatenate([expand_out(c_c_re.astype(F32)),
                                   -expand_out(c_c_im.astype(F32))], axis=0).astype(BF16)
    p["c_are"] = _row(abar_re.reshape(C_GROUPS, C_GROUP, C_STATE)[:, 0, :])
    p["c_aim"] = _row(abar_im.reshape(C_GROUPS, C_GROUP, C_STATE)[:, 0, :])
    p["c_d"] = _row(c_d)
    p["c_w_glu"] = c_w_glu.astype(BF16)
    p["c_b_glu"] = _row(c_b_glu)

    p["m_norm_g"] = _row(m_norm_g)
    p["m_w_kv"] = m_w_kv.astype(BF16)
    p["m_gq"] = _row(jnp.tile(m_qk_g_q, M_HEADS))
    p["m_gk"] = _row(jnp.tile(m_qk_g_k, M_HEADS))
    p["w_br_a"] = w_br_a.astype(BF16)
    p["w_br_b"] = w_br_b.astype(BF16)
    p["w_br_c"] = w_br_c.astype(BF16)
    p["w_br_m"] = w_br_m.astype(BF16)
    p["w_out"] = w_out.astype(BF16)
    return p


def kernel(x, mem, positions, norm_g, w_in, b_merge, a_norm_g, a_w_s, a_b_s, b_q_norm_g, b_kv_norm_g, b_w_uq, b_w_ukv, b_qk_g_q, b_qk_g_k, c_a_re, c_a_im, c_log_dt, c_b_re, c_b_im, c_c_re, c_c_im, c_d, c_w_glu, c_b_glu, m_norm_g, m_w_kv, m_qk_g_q, m_qk_g_k, w_br_a, w_br_b, w_br_c, w_br_m, w_out):
    bsz, seq, d_model = x.shape
    mem_len = mem.shape[1]
    depth = norm_g.shape[0]
    assert d_model == D_MODEL and seq % TOKEN_TILE == 0 and seq % ATTN_TILE == 0
    assert seq % S5_CHUNK == 0 and bsz == 8

    stacked = (norm_g, w_in, b_merge, a_norm_g, a_w_s, a_b_s, b_q_norm_g, b_kv_norm_g,
               b_w_uq, b_w_ukv, b_qk_g_q, b_qk_g_k, c_a_re, c_a_im, c_log_dt, c_b_re,
               c_b_im, c_c_re, c_c_im, c_d, c_w_glu, c_b_glu, m_norm_g, m_w_kv, m_qk_g_q,
               m_qk_g_k, w_br_a, w_br_b, w_br_c, w_br_m, w_out)

    tabs = _rope_tables(positions)
    x2 = x.reshape(bsz * seq, d_model)
    mem2 = mem.reshape(bsz * mem_len, d_model)
    for layer in range(depth):
        p = _prep_layer(*(t[layer] for t in stacked))
        ya, q, k, vt, bg, ci, cg, mq, mg = _inproj(x2, p, tabs, bsz, seq)
        yb = _attention(q, k, vt, bg, bsz, seq)
        yc = _s5(ci, cg, p, bsz, seq)
        mk, mv = _memkv(mem2, p, bsz, mem_len)
        x2 = _merge(x2, ya, yb, yc, mq, mg, mk, mv, p, bsz, seq, mem_len)
    return x2.reshape(bsz, seq, d_model)
```

```python
import functools
import math

import jax
import jax.numpy as jnp
from jax import lax
from jax.experimental import pallas as pl
from jax.experimental.pallas import tpu as pltpu

F32 = jnp.float32
BF16 = jnp.bfloat16

D_MODEL = 1024
EPS = 1e-6
N_BRANCH = 4

A_WIDTH = 256
A_GROUPS = 4
A_GROUP_DIM = A_WIDTH // A_GROUPS
A_CHUNK = 128

B_HEADS = 8
B_NOPE = 64
B_ROPE = 32
B_QK_DIM = B_NOPE + B_ROPE
B_VDIM = 64
B_Q_RANK = 768
B_KV_RANK = 256
B_WIDTH = B_HEADS * B_VDIM
ROPE_THETA = 10000.0
HEAD_PAD = 128
QK_PAD = B_HEADS * HEAD_PAD

C_WIDTH = 256
C_GROUP = 16
C_GROUPS = C_WIDTH // C_GROUP
C_STATE = 64
C_STATES = C_GROUPS * C_STATE

M_HEADS = 4
M_HEAD_DIM = 64
M_WIDTH = M_HEADS * M_HEAD_DIM

COL_CKV = 0
COL_KPE = 256
COL_CQ = 384
COL_AV = 1152
COL_AU = 1408
COL_AG = 1664
COL_BG = 1920
COL_CI = 2432
COL_CG = 2688
COL_MQ = 2944
COL_MG = 3200
N_SMALL = 3456

V7X_VMEM_LIMIT_BYTES = 56 * 1024 * 1024

TOKEN_TILE = 256
ATTN_TILE = 256
S5_CHUNK = 64
S5_LANE_BLOCK = 256


def _sigmoid(x):
    return 1.0 / (1.0 + jnp.exp(-x))


def _silu(x):
    return x * _sigmoid(x)


def _rms(x, g):
    return x * lax.rsqrt(jnp.mean(x * x, axis=-1, keepdims=True) + EPS) * g


def _dot(a, b):
    return jnp.dot(a, b, preferred_element_type=F32)


def _dot_nt(a, b):
    return lax.dot_general(a, b, (((1,), (1,)), ((), ())), preferred_element_type=F32)


ROPE_HALF = B_ROPE // 2
ROPE_LO = 0
ROPE_HI = HEAD_PAD // 2


def _head_layout(w):
    nope, rope = w[..., :B_NOPE], w[..., B_NOPE:]
    split = ROPE_HI - ROPE_HALF
    pad = jnp.zeros(w.shape[:-1] + (HEAD_PAD - B_QK_DIM,), w.dtype)
    return jnp.concatenate([rope[..., :ROPE_HALF], nope[..., :split], rope[..., ROPE_HALF:],
                            nope[..., split:], pad], axis=-1)


def _rope_table_body(pos_col_ref, pos_row_ref, freq_row_ref, freq_col_ref,
                     c_ref, s_ref, ct_ref, st_ref):
    ang = pos_col_ref[...] * freq_row_ref[...]
    lane = lax.broadcasted_iota(jnp.int32, ang.shape, 1)
    lo = lane < ROPE_LO + ROPE_HALF
    hi = (lane >= ROPE_HI) & (lane < ROPE_HI + ROPE_HALF)
    cos = jnp.cos(ang)
    sin = jnp.sin(ang)
    c_ref[...] = jnp.where(lo | hi, cos, 1.0)
    s_ref[...] = jnp.where(lo, -sin, jnp.where(hi, sin, 0.0))
    ang_t = freq_col_ref[...] * pos_row_ref[...]
    ct_ref[...] = jnp.cos(ang_t)
    st_ref[...] = jnp.sin(ang_t)


def _rope_tables(positions):
    n_tok = positions.size
    inv_freq = ROPE_THETA ** (-jnp.arange(ROPE_HALF, dtype=F32) / ROPE_HALF)
    freq_row = _head_layout(jnp.concatenate([jnp.zeros((B_NOPE,), F32), inv_freq, inv_freq]))
    tm = 1024
    freq_col = jnp.broadcast_to(inv_freq[:, None], (ROPE_HALF, tm))
    pos = positions.astype(F32)
    tab = jax.ShapeDtypeStruct((n_tok, HEAD_PAD), F32)
    tab_t = jax.ShapeDtypeStruct((ROPE_HALF, n_tok), F32)
    return pl.pallas_call(
        _rope_table_body,
        out_shape=(tab, tab, tab_t, tab_t),
        grid=(n_tok // tm,),
        in_specs=[pl.BlockSpec((tm, 1), lambda i: (i, 0)),
                  pl.BlockSpec((1, tm), lambda i: (0, i)),
                  pl.BlockSpec((1, HEAD_PAD), lambda i: (0, 0)),
                  pl.BlockSpec((ROPE_HALF, tm), lambda i: (0, 0))],
        out_specs=(pl.BlockSpec((tm, HEAD_PAD), lambda i: (i, 0)),) * 2
        + (pl.BlockSpec((ROPE_HALF, tm), lambda i: (0, i)),) * 2,
        compiler_params=pltpu.CompilerParams(dimension_semantics=("parallel",)),
        name="rope_tables",
    )(pos.reshape(n_tok, 1), pos.reshape(1, n_tok), freq_row.reshape(1, HEAD_PAD), freq_col)


def _inproj_body(x_ref, ng_ref, w_ref, ang_ref, wst_ref, bias_ref, qng_ref, kvng_ref,
                 wuqt_ref, wuk_ref, wuvt_ref, gq_ref, gk_ref, c_ref, s_ref, ct_ref, st_ref,
                 ya_ref, qt_ref, k_ref, vt_ref, bg_ref, ci_ref, cg_ref, mq_ref, mg_ref):
    tm = x_ref.shape[0]
    h = _rms(x_ref[...], ng_ref[...]).astype(BF16)

    z_lat = _dot(h, w_ref[:, :COL_AV])

    def lat(lo, width):
        return z_lat[:, lo:lo + width]

    ckv = _rms(lat(COL_CKV, B_KV_RANK), kvng_ref[...]).astype(BF16)
    kf = _dot(ckv, wuk_ref[...])
    z_rest = _dot(h, w_ref[:, COL_AV:])

    def proj(lo, width):
        return z_rest[:, lo - COL_AV:lo - COL_AV + width]

    kpe = lat(COL_KPE, HEAD_PAD)
    c_tab, s_tab = c_ref[...], s_ref[...]
    for hd in range(B_HEADS):
        lanes = slice(hd * HEAD_PAD, (hd + 1) * HEAD_PAD)
        kh = kf[:, lanes] + kpe
        ssq = jnp.sum(kh * kh, axis=-1, keepdims=True)
        kh = kh * lax.rsqrt(ssq * (1.0 / B_QK_DIM) + EPS) * gk_ref[...]
        k_ref[:, lanes] = (kh * c_tab + pltpu.roll(kh, HEAD_PAD // 2, 1) * s_tab).astype(BF16)
    vt_ref[...] = _dot_nt(wuvt_ref[...], ckv).astype(BF16)

    u = jax.nn.gelu(proj(COL_AU, A_WIDTH))
    v = _rms(jax.nn.gelu(proj(COL_AV, A_WIDTH)), ang_ref[...]).astype(BF16)
    ag = _silu(proj(COL_AG, A_WIDTH))
    lane_group = lax.broadcasted_iota(jnp.int32, (A_CHUNK, A_WIDTH), 1) // A_GROUP_DIM
    for c in range(tm // A_CHUNK):
        rows = slice(c * A_CHUNK, (c + 1) * A_CHUNK)
        s_all = _dot(wst_ref[...], v[rows, :])
        s = s_all[(A_GROUPS - 1) * A_CHUNK:, :]
        for g in range(A_GROUPS - 2, -1, -1):
            s = jnp.where(lane_group == g, s_all[g * A_CHUNK:(g + 1) * A_CHUNK, :], s)
        s = s + bias_ref[...]
        ya_ref[rows, :] = (u[rows, :] * s * ag[rows, :]).astype(BF16)

    q_scale = B_QK_DIM ** -0.5 * math.log2(math.e)
    cq = _rms(lat(COL_CQ, B_Q_RANK), qng_ref[...]).astype(BF16)
    half_rows = QK_PAD // 2
    qt_halves = [_dot_nt(wuqt_ref[i * half_rows:(i + 1) * half_rows, :], cq) for i in range(2)]
    cos_t, sin_t = ct_ref[...], st_ref[...]
    for hd in range(B_HEADS):
        r0 = hd * HEAD_PAD
        qr = r0 % half_rows
        qh = qt_halves[r0 // half_rows][qr:qr + HEAD_PAD, :]
        ssq = jnp.sum(qh * qh, axis=0, keepdims=True)
        qh = qh * (lax.rsqrt(ssq * (1.0 / B_QK_DIM) + EPS) * q_scale) * gq_ref[...]
        x1 = qh[ROPE_LO:ROPE_LO + ROPE_HALF, :]
        x2 = qh[ROPE_HI:ROPE_HI + ROPE_HALF, :]
        qt_ref[r0:r0 + HEAD_PAD, :] = qh.astype(BF16)
        qt_ref[r0 + ROPE_LO:r0 + ROPE_LO + ROPE_HALF, :] = (x1 * cos_t - x2 * sin_t).astype(BF16)
        qt_ref[r0 + ROPE_HI:r0 + ROPE_HI + ROPE_HALF, :] = (x2 * cos_t + x1 * sin_t).astype(BF16)

    bg_ref[...] = _silu(proj(COL_BG, B_WIDTH))
    ci_ref[...] = proj(COL_CI, C_WIDTH)
    cg_ref[...] = _silu(proj(COL_CG, C_WIDTH))
    mq_ref[...] = proj(COL_MQ, M_WIDTH)
    mg_ref[...] = _silu(proj(COL_MG, M_WIDTH))


def _inproj(x2, p, tabs, bsz, seq):
    n_tok = x2.shape[0]
    tm = TOKEN_TILE
    nst = seq // tm

    def tok(i):
        return (i, 0)

    def seq_major(i):
        return (i % nst, i // nst)

    def const(i):
        return (0, 0)

    def full(a):
        return pl.BlockSpec(a.shape, const)

    weights = (p["norm_g"], p["w_small"], p["a_norm_g"], p["a_wst"], p["a_bias"],
               p["b_q_norm_g"], p["b_kv_norm_g"], p["b_wuqt"], p["b_wuk"], p["b_wuvt"],
               p["b_gq"], p["b_gk"])
    in_specs = ([pl.BlockSpec((tm, D_MODEL), tok)] + [full(w) for w in weights]
                + [pl.BlockSpec((tm, HEAD_PAD), tok)] * 2
                + [pl.BlockSpec((ROPE_HALF, tm), lambda i: (0, i))] * 2)
    out_shape = (
        jax.ShapeDtypeStruct((n_tok, A_WIDTH), BF16),
        jax.ShapeDtypeStruct((n_tok // tm, QK_PAD, tm), BF16),
        jax.ShapeDtypeStruct((n_tok, QK_PAD), BF16),
        jax.ShapeDtypeStruct((n_tok // tm, B_WIDTH, tm), BF16),
        jax.ShapeDtypeStruct((n_tok, B_WIDTH), F32),
        jax.ShapeDtypeStruct((seq, bsz * C_WIDTH), F32),
        jax.ShapeDtypeStruct((seq, bsz * C_WIDTH), F32),
        jax.ShapeDtypeStruct((n_tok, M_WIDTH), F32),
        jax.ShapeDtypeStruct((n_tok, M_WIDTH), F32),
    )
    out_specs = (
        pl.BlockSpec((tm, A_WIDTH), tok),
        pl.BlockSpec((None, QK_PAD, tm), lambda i: (i, 0, 0)),
        pl.BlockSpec((tm, QK_PAD), tok),
        pl.BlockSpec((None, B_WIDTH, tm), lambda i: (i, 0, 0)),
        pl.BlockSpec((tm, B_WIDTH), tok),
        pl.BlockSpec((tm, C_WIDTH), seq_major),
        pl.BlockSpec((tm, C_WIDTH), seq_major),
        pl.BlockSpec((tm, M_WIDTH), tok),
        pl.BlockSpec((tm, M_WIDTH), tok),
    )
    return pl.pallas_call(
        _inproj_body,
        out_shape=out_shape,
        grid=(n_tok // tm,),
        in_specs=in_specs,
        out_specs=out_specs,
        compiler_params=pltpu.CompilerParams(
            dimension_semantics=("parallel",), vmem_limit_bytes=V7X_VMEM_LIMIT_BYTES),
        name="inproj",
    )(x2, *weights, *tabs)


def _attn_body(qt_ref, k_ref, vt_ref, g_ref, o_ref, m_ref, l_ref, acc_ref, s_ref):
    tq = qt_ref.shape[1]
    qi = pl.program_id(1)
    key_le_query = (lax.broadcasted_iota(jnp.int32, (tq, tq), 0)
                    <= lax.broadcasted_iota(jnp.int32, (tq, tq), 1))
    m_ref[...] = jnp.full(m_ref.shape, -jnp.inf, F32)
    l_ref[...] = jnp.zeros(l_ref.shape, F32)
    acc_ref[...] = jnp.zeros(acc_ref.shape, F32)

    def block(j, masked, slot):
        r0 = pl.multiple_of(j * tq, tq)
        alphas = []
        for hd in range(B_HEADS):
            lanes = slice(hd * HEAD_PAD, (hd + 1) * HEAD_PAD)
            st = _dot(k_ref[pl.ds(r0, tq), lanes], qt_ref[lanes, :])
            if masked:
                st = jnp.where(key_le_query, st, -jnp.inf)
            s_ref[slot, hd] = st
            m_old = m_ref[hd:hd + 1, :]
            m_new = jnp.maximum(m_old, jnp.max(st, axis=0, keepdims=True))
            m_ref[hd:hd + 1, :] = m_new
            alphas.append(jnp.exp2(m_old - m_new))
        for hd in range(B_HEADS):
            rows = slice(hd * B_VDIM, (hd + 1) * B_VDIM)
            pt = jnp.exp2(s_ref[slot, hd] - m_ref[hd:hd + 1, :])
            l_ref[hd:hd + 1, :] = (alphas[hd] * l_ref[hd:hd + 1, :]
                                   + jnp.sum(pt, axis=0, keepdims=True))
            acc_ref[rows, :] = (alphas[hd] * acc_ref[rows, :]
                                + _dot(vt_ref[j, rows, :], pt.astype(BF16)))

    def block_pair(i, carry):
        block(2 * i, False, 0)
        block(2 * i + 1, False, 1)
        return carry

    lax.fori_loop(0, qi // 2, block_pair, 0)

    @pl.when(qi % 2 == 1)
    def _():
        block(qi - 1, False, 0)

    block(qi, True, 1)

    for hd in range(B_HEADS):
        rows = slice(hd * B_VDIM, (hd + 1) * B_VDIM)
        acc_ref[rows, :] = acc_ref[rows, :] * (1.0 / l_ref[hd:hd + 1, :])
    o_ref[...] = (acc_ref[...].T * g_ref[...]).astype(BF16)


def _attention(q, k, vt, gate, bsz, seq):
    tq = ATTN_TILE
    nq = seq // tq
    return pl.pallas_call(
        _attn_body,
        out_shape=jax.ShapeDtypeStruct((bsz * seq, B_WIDTH), BF16),
        grid=(bsz, nq),
        in_specs=[pl.BlockSpec((None, QK_PAD, tq), lambda b, i: (b * nq + i, 0, 0)),
                  pl.BlockSpec((seq, QK_PAD), lambda b, i: (b, 0)),
                  pl.BlockSpec((nq, B_WIDTH, tq), lambda b, i: (b, 0, 0)),
                  pl.BlockSpec((tq, B_WIDTH), lambda b, i: (b * nq + i, 0))],
        out_specs=pl.BlockSpec((tq, B_WIDTH), lambda b, i: (b * nq + i, 0)),
        scratch_shapes=[pltpu.VMEM((B_HEADS, tq), F32), pltpu.VMEM((B_HEADS, tq), F32),
                        pltpu.VMEM((B_WIDTH, tq), F32), pltpu.VMEM((2, B_HEADS, tq, tq), F32)],
        compiler_params=pltpu.CompilerParams(
            dimension_semantics=("parallel", "parallel"),
            vmem_limit_bytes=V7X_VMEM_LIMIT_BYTES),
        name="latent_attention",
    )(q, k, vt, gate)


def _s5_discretise_body(are_ref, aim_ref, ldt_ref, bre_ref, bim_ref,
                        abar_re_ref, abar_im_ref, bbar_re_ref, bbar_im_ref):
    a_re, a_im = are_ref[...], aim_ref[...]
    dt = jnp.exp(ldt_ref[...])
    mag = jnp.exp(a_re * dt)
    abar_re = mag * jnp.cos(a_im * dt)
    abar_im = mag * jnp.sin(a_im * dt)
    num_re = abar_re - 1.0
    inv_den = 1.0 / (a_re * a_re + a_im * a_im)
    q_re = (num_re * a_re + abar_im * a_im) * inv_den
    q_im = (abar_im * a_re - num_re * a_im) * inv_den
    b_re, b_im = bre_ref[...], bim_ref[...]
    abar_re_ref[...] = abar_re
    abar_im_ref[...] = abar_im
    bbar_re_ref[...] = q_re * b_re - q_im * b_im
    bbar_im_ref[...] = q_re * b_im + q_im * b_re


def _s5_discretise(a_re, a_im, log_dt, b_re, b_im):
    out = jax.ShapeDtypeStruct(a_re.shape, F32)
    return pl.pallas_call(_s5_discretise_body, out_shape=(out,) * 4, name="s5_discretise")(
        a_re, a_im, log_dt, b_re, b_im)


def _s5_body(u_ref, cg_ref, bmat_ref, cmat_ref, are_ref, aim_ref, d_ref, wglu_ref, bglu_ref,
             y_ref, state_ref, buf_ref):
    lt, bsz, width = u_ref.shape
    rows = lt * bsz
    lb_w = S5_LANE_BLOCK

    @pl.when(pl.program_id(0) == 0)
    def _():
        state_ref[...] = jnp.zeros_like(state_ref)

    u2 = u_ref[...].reshape(rows, width)
    ub = u2.astype(BF16)
    half = rows // 2
    y_halves = [None, None]
    for lb in range(C_STATES // lb_w):
        cols = slice(2 * lb * lb_w, 2 * (lb + 1) * lb_w)
        re_l = slice(2 * lb * lb_w, (2 * lb + 1) * lb_w)
        im_l = slice((2 * lb + 1) * lb_w, 2 * (lb + 1) * lb_w)
        buf_ref[:, cols] = _dot(ub, bmat_ref[:, cols])
        a_re = jnp.broadcast_to(are_ref[:, lb * lb_w:(lb + 1) * lb_w], (bsz, lb_w))
        a_im = jnp.broadcast_to(aim_ref[:, lb * lb_w:(lb + 1) * lb_w], (bsz, lb_w))
        s_re, s_im = state_ref[:, re_l], state_ref[:, im_l]
        for t in range(lt):
            r = slice(t * bsz, (t + 1) * bsz)
            n_re = a_re * s_re - a_im * s_im + buf_ref[r, re_l]
            n_im = a_re * s_im + a_im * s_re + buf_ref[r, im_l]
            buf_ref[r, re_l] = n_re
            buf_ref[r, im_l] = n_im
            s_re, s_im = n_re, n_im
        state_ref[:, re_l] = s_re
        state_ref[:, im_l] = s_im
        for hf in range(2):
            r = slice(hf * half, (hf + 1) * half)
            part = _dot(buf_ref[r, cols].astype(BF16), cmat_ref[cols, :])
            y_halves[hf] = part if y_halves[hf] is None else y_halves[hf] + part

    y = jnp.concatenate(y_halves, axis=0) + d_ref[...] * u2
    y = jax.nn.gelu(y)
    y = y * _sigmoid(_dot(y.astype(BF16), wglu_ref[...]) + bglu_ref[...])
    y_ref[...] = (y * cg_ref[...].reshape(rows, width)).reshape(lt, bsz, width)


def _s5(c_in, c_gate, p, bsz, seq):
    lt = S5_CHUNK
    u3 = c_in.reshape(seq, bsz, C_WIDTH)
    g3 = c_gate.reshape(seq, bsz, C_WIDTH)

    def chunk(i):
        return (i, 0, 0)

    def const(i):
        return (0, 0)

    weights = (p["c_bmat"], p["c_cmat"], p["c_are"], p["c_aim"], p["c_d"], p["c_w_glu"],
               p["c_b_glu"])
    y = pl.pallas_call(
        _s5_body,
        out_shape=jax.ShapeDtypeStruct((seq, bsz, C_WIDTH), F32),
        grid=(seq // lt,),
        in_specs=[pl.BlockSpec((lt, bsz, C_WIDTH), chunk)] * 2
        + [pl.BlockSpec(w.shape, const) for w in weights],
        out_specs=pl.BlockSpec((lt, bsz, C_WIDTH), chunk),
        scratch_shapes=[pltpu.VMEM((bsz, 2 * C_STATES), F32),
                        pltpu.VMEM((lt * bsz, 2 * C_STATES), F32)],
        compiler_params=pltpu.CompilerParams(
            dimension_semantics=("arbitrary",), vmem_limit_bytes=V7X_VMEM_LIMIT_BYTES),
        name="s5_scan",
    )(u3, g3, *weights)
    return y.reshape(seq, bsz * C_WIDTH)


def _head_rms(x, gain, n_heads, head_dim):
    head_id = lax.broadcasted_iota(jnp.int32, x.shape, 1) // head_dim
    out = jnp.zeros_like(x)
    for hd in range(n_heads):
        xh = jnp.where(head_id == hd, x, 0.0)
        ssq = jnp.sum(xh * xh, axis=-1, keepdims=True)
        out = out + xh * lax.rsqrt(ssq * (1.0 / head_dim) + EPS)
    return out * gain


def _memkv_body(mem_ref, ng_ref, wkv_ref, gk_ref, k_ref, v_ref):
    mem_len = mem_ref.shape[0]
    mh = _rms(mem_ref[...], ng_ref[...]).astype(BF16)
    kv = _dot(mh, wkv_ref[...])
    k = _head_rms(kv[:, :M_WIDTH], gk_ref[...], M_HEADS, M_HEAD_DIM)
    v = kv[:, M_WIDTH:]
    head_id = lax.broadcasted_iota(jnp.int32, k.shape, 1) // M_HEAD_DIM
    for hd in range(M_HEADS):
        rows = slice(hd * mem_len, (hd + 1) * mem_len)
        k_ref[rows, :] = jnp.where(head_id == hd, k, 0.0).astype(BF16)
        v_ref[rows, :] = jnp.where(head_id == hd, v, 0.0).astype(BF16)


def _memkv(mem2, p, bsz, mem_len):
    def const(b):
        return (0, 0)

    weights = (p["m_norm_g"], p["m_w_kv"], p["m_gk"])
    out = jax.ShapeDtypeStruct((bsz * M_HEADS * mem_len, M_WIDTH), BF16)
    return pl.pallas_call(
        _memkv_body,
        out_shape=(out, out),
        grid=(bsz,),
        in_specs=[pl.BlockSpec((mem_len, D_MODEL), lambda b: (b, 0))]
        + [pl.BlockSpec(w.shape, const) for w in weights],
        out_specs=(pl.BlockSpec((M_HEADS * mem_len, M_WIDTH), lambda b: (b, 0)),) * 2,
        compiler_params=pltpu.CompilerParams(dimension_semantics=("parallel",)),
        name="memory_kv",
    )(mem2, *weights)


def _merge_body(x_ref, ng_ref, wm_ref, bm_ref, ya_ref, yb_ref, yc_ref, mq_ref, mg_ref,
                mk_ref, mv_ref, gq_ref, wa_ref, wb_ref, wc_ref, wmm_ref, wo_ref, o_ref):
    x = x_ref[...]
    h = _rms(x, ng_ref[...]).astype(BF16)

    def gated(br, y, w_ref):
        cols = slice(br * D_MODEL, (br + 1) * D_MODEL)
        gate = _sigmoid(_dot(h, wm_ref[:, cols]) + bm_ref[:, cols])
        return gate * _dot(y, w_ref[...])

    mq = mq_ref[...]
    mem_len = mk_ref.shape[0] // M_HEADS
    same_head = (lax.broadcasted_iota(jnp.int32, (M_WIDTH, M_WIDTH), 0) // M_HEAD_DIM
                 == lax.broadcasted_iota(jnp.int32, (M_WIDTH, M_WIDTH), 1) // M_HEAD_DIM)
    head_ones = jnp.where(same_head, 1.0, 0.0).astype(BF16)
    q_sq = mq * mq
    q_sq_hi = q_sq.astype(BF16)
    q_sq_lo = (q_sq - q_sq_hi.astype(F32)).astype(BF16)
    ssq = _dot(q_sq_hi, head_ones) + _dot(q_sq_lo, head_ones)
    merged = gated(0, ya_ref[...], wa_ref)
    qn = (mq * lax.rsqrt(ssq * (1.0 / M_HEAD_DIM) + EPS) * gq_ref[...]).astype(BF16)
    s_all = _dot_nt(qn, mk_ref[...])
    merged = merged + gated(1, yb_ref[...], wb_ref)
    probs = []
    for hd in range(M_HEADS):
        s = s_all[:, hd * mem_len:(hd + 1) * mem_len]
        e = jnp.exp(s - jnp.max(s, axis=-1, keepdims=True))
        probs.append((e * (1.0 / jnp.sum(e, axis=-1, keepdims=True))).astype(BF16))
    om = _dot(jnp.concatenate(probs, axis=1), mv_ref[...])
    merged = merged + gated(2, yc_ref[...].astype(BF16), wc_ref)
    ym = (om * mg_ref[...]).astype(BF16)
    merged = merged + gated(3, ym, wmm_ref)
    o_ref[...] = x + _dot(merged.astype(BF16), wo_ref[...])


def _merge(x2, ya, yb, yc, mq, mg, mk, mv, p, bsz, seq, mem_len):
    n_tok = x2.shape[0]
    tm = TOKEN_TILE
    nst = seq // tm

    def tok(i):
        return (i, 0)

    def seq_major(i):
        return (i % nst, i // nst)

    def per_batch(i):
        return (i // nst, 0)

    def const(i):
        return (0, 0)

    def resident(a):
        return pl.BlockSpec(a.shape, const, pipeline_mode=pl.Buffered(1))

    in_specs = [
        pl.BlockSpec((tm, D_MODEL), tok),
        resident(p["norm_g"]), resident(p["w_merge"]), resident(p["b_merge"]),
        pl.BlockSpec((tm, A_WIDTH), tok),
        pl.BlockSpec((tm, B_WIDTH), tok),
        pl.BlockSpec((tm, C_WIDTH), seq_major),
        pl.BlockSpec((tm, M_WIDTH), tok),
        pl.BlockSpec((tm, M_WIDTH), tok),
        pl.BlockSpec((M_HEADS * mem_len, M_WIDTH), per_batch),
        pl.BlockSpec((M_HEADS * mem_len, M_WIDTH), per_batch),
        resident(p["m_gq"]), resident(p["w_br_a"]), resident(p["w_br_b"]),
        resident(p["w_br_c"]), resident(p["w_br_m"]), resident(p["w_out"]),
    ]
    return pl.pallas_call(
        _merge_body,
        out_shape=jax.ShapeDtypeStruct((n_tok, D_MODEL), F32),
        grid=(n_tok // tm,),
        in_specs=in_specs,
        out_specs=pl.BlockSpec((tm, D_MODEL), tok),
        compiler_params=pltpu.CompilerParams(
            dimension_semantics=("parallel",), vmem_limit_bytes=V7X_VMEM_LIMIT_BYTES),
        name="merge",
    )(x2, p["norm_g"], p["w_merge"], p["b_merge"], ya, yb, yc, mq, mg, mk, mv,
      p["m_gq"], p["w_br_a"], p["w_br_b"], p["w_br_c"], p["w_br_m"], p["w_out"])


def _row(v):
    return v.reshape(1, -1).astype(F32)


def _prep_layer(norm_g, w_in, b_merge, a_norm_g, a_w_s, a_b_s, b_q_norm_g, b_kv_norm_g,
                b_w_uq, b_w_ukv, b_qk_g_q, b_qk_g_k, c_a_re, c_a_im, c_log_dt, c_b_re,
                c_b_im, c_c_re, c_c_im, c_d, c_w_glu, c_b_glu, m_norm_g, m_w_kv, m_qk_g_q,
                m_qk_g_k, w_br_a, w_br_b, w_br_c, w_br_m, w_out):
    p = {}
    p["norm_g"] = _row(norm_g)
    kpe = _head_layout(jnp.pad(w_in[:, 1792:1824], ((0, 0), (B_NOPE, 0))))
    p["w_small"] = jnp.concatenate(
        [w_in[:, 1536:1792], kpe, w_in[:, 768:1536], w_in[:, 256:512], w_in[:, :256],
         w_in[:, 512:768], w_in[:, 1824:3360]], axis=1).astype(BF16)
    p["w_merge"] = w_in[:, 3360:].astype(BF16)
    p["b_merge"] = _row(b_merge)

    p["a_norm_g"] = _row(a_norm_g)
    causal = jnp.tril(jnp.ones((A_CHUNK, A_CHUNK), dtype=bool))
    p["a_wst"] = jnp.where(causal, a_w_s, 0.0).reshape(A_GROUPS * A_CHUNK, A_CHUNK).astype(BF16)
    p["a_bias"] = jnp.repeat(a_b_s.T, A_GROUP_DIM, axis=1).astype(F32)

    p["b_q_norm_g"] = _row(b_q_norm_g)
    p["b_kv_norm_g"] = _row(b_kv_norm_g)
    wuq = _head_layout(b_w_uq.reshape(B_Q_RANK, B_HEADS, B_QK_DIM))
    p["b_wuqt"] = wuq.reshape(B_Q_RANK, QK_PAD).T.astype(BF16)
    wukv = b_w_ukv.reshape(B_KV_RANK, B_HEADS, B_NOPE + B_VDIM)
    wuk = _head_layout(jnp.pad(wukv[:, :, :B_NOPE], ((0, 0), (0, 0), (0, B_ROPE))))
    p["b_wuk"] = wuk.reshape(B_KV_RANK, QK_PAD).astype(BF16)
    p["b_wuvt"] = wukv[:, :, B_NOPE:].reshape(B_KV_RANK, B_WIDTH).T.astype(BF16)
    p["b_gq"] = jnp.broadcast_to(_head_layout(b_qk_g_q.astype(F32))[:, None],
                                 (HEAD_PAD, TOKEN_TILE))
    p["b_gk"] = _row(_head_layout(b_qk_g_k))

    def rep(a):
        return jnp.repeat(a.astype(F32), C_GROUP, axis=0)

    def gcp(b):
        return b.astype(F32).transpose(0, 2, 1).reshape(C_WIDTH, C_STATE)

    log_dt = jnp.broadcast_to(c_log_dt.astype(F32)[:, None], (C_GROUPS, C_STATE))
    abar_re, abar_im, bbar_re, bbar_im = _s5_discretise(
        rep(c_a_re), rep(c_a_im), rep(log_dt), gcp(c_b_re), gcp(c_b_im))
    eye = jnp.eye(C_GROUPS, dtype=F32)

    def expand_in(m):
        m = m.reshape(C_GROUPS, C_GROUP, C_STATE)
        return (eye[:, None, :, None] * m[:, :, None, :]).reshape(C_WIDTH, C_STATES)

    def expand_out(m):
        return (eye[:, None, :, None] * m.transpose(0, 2, 1)[:, :, None, :]).reshape(
            C_STATES, C_WIDTH)

    n_lb = C_STATES // S5_LANE_BLOCK
    bmat = jnp.stack([expand_in(bbar_re).reshape(C_WIDTH, n_lb, S5_LANE_BLOCK),
                      expand_in(bbar_im).reshape(C_WIDTH, n_lb, S5_LANE_BLOCK)], axis=2)
    p["c_bmat"] = bmat.reshape(C_WIDTH, 2 * C_STATES).astype(BF16)
    cmat = jnp.stack([expand_out(c_c_re.astype(F32)).reshape(n_lb, S5_LANE_BLOCK, C_WIDTH),
                      -expand_out(c_c_im.astype(F32)).reshape(n_lb, S5_LANE_BLOCK, C_WIDTH)],
                     axis=1)
    p["c_cmat"] = cmat.reshape(2 * C_STATES, C_WIDTH).astype(BF16)
    p["c_are"] = _row(abar_re.reshape(C_GROUPS, C_GROUP, C_STATE)[:, 0, :])
    p["c_aim"] = _row(abar_im.reshape(C_GROUPS, C_GROUP, C_STATE)[:, 0, :])
    p["c_d"] = _row(c_d)
    p["c_w_glu"] = c_w_glu.astype(BF16)
    p["c_b_glu"] = _row(c_b_glu)

    p["m_norm_g"] = _row(m_norm_g)
    p["m_w_kv"] = m_w_kv.astype(BF16)
    p["m_gq"] = _row(jnp.tile(m_qk_g_q, M_HEADS)) * (M_HEAD_DIM ** -0.5)
    p["m_gk"] = _row(jnp.tile(m_qk_g_k, M_HEADS))
    p["w_br_a"] = w_br_a.astype(BF16)
    p["w_br_b"] = w_br_b.astype(BF16)
    p["w_br_c"] = w_br_c.astype(BF16)
    p["w_br_m"] = w_br_m.astype(BF16)
    p["w_out"] = w_out.astype(BF16)
    return p


def kernel(x, mem, positions, norm_g, w_in, b_merge, a_norm_g, a_w_s, a_b_s, b_q_norm_g, b_kv_norm_g, b_w_uq, b_w_ukv, b_qk_g_q, b_qk_g_k, c_a_re, c_a_im, c_log_dt, c_b_re, c_b_im, c_c_re, c_c_im, c_d, c_w_glu, c_b_glu, m_norm_g, m_w_kv, m_qk_g_q, m_qk_g_k, w_br_a, w_br_b, w_br_c, w_br_m, w_out):
    bsz, seq, d_model = x.shape
    mem_len = mem.shape[1]
    depth = norm_g.shape[0]
    assert d_model == D_MODEL and seq % TOKEN_TILE == 0 and seq % ATTN_TILE == 0
    assert seq % S5_CHUNK == 0 and bsz == 8 and TOKEN_TILE == ATTN_TILE

    stacked = (norm_g, w_in, b_merge, a_norm_g, a_w_s, a_b_s, b_q_norm_g, b_kv_norm_g,
               b_w_uq, b_w_ukv, b_qk_g_q, b_qk_g_k, c_a_re, c_a_im, c_log_dt, c_b_re,
               c_b_im, c_c_re, c_c_im, c_d, c_w_glu, c_b_glu, m_norm_g, m_w_kv, m_qk_g_q,
               m_qk_g_k, w_br_a, w_br_b, w_br_c, w_br_m, w_out)

    tabs = _rope_tables(positions)
    x2 = x.reshape(bsz * seq, d_model)
    mem2 = mem.reshape(bsz * mem_len, d_model)
    for layer in range(depth):
        p = _prep_layer(*(t[layer] for t in stacked))
        ya, q, k, vt, bg, ci, cg, mq, mg = _inproj(x2, p, tabs, bsz, seq)
        yb = _attention(q, k, vt, bg, bsz, seq)
        yc = _s5(ci, cg, p, bsz, seq)
        mk, mv = _memkv(mem2, p, bsz, mem_len)
        x2 = _merge(x2, ya, yb, yc, mq, mg, mk, mv, p, bsz, seq, mem_len)
    return x2.reshape(bsz, seq, d_model)
```

```python
import functools
import math

import jax
import jax.numpy as jnp
from jax import lax
from jax.experimental import pallas as pl
from jax.experimental.pallas import tpu as pltpu

F32 = jnp.float32
BF16 = jnp.bfloat16

D_MODEL = 1024
EPS = 1e-6
N_BRANCH = 4

A_WIDTH = 256
A_GROUPS = 4
A_GROUP_DIM = A_WIDTH // A_GROUPS
A_CHUNK = 128

B_HEADS = 8
B_NOPE = 64
B_ROPE = 32
B_QK_DIM = B_NOPE + B_ROPE
B_VDIM = 64
B_Q_RANK = 768
B_KV_RANK = 256
B_WIDTH = B_HEADS * B_VDIM
ROPE_THETA = 10000.0
HEAD_PAD = 128
QK_PAD = B_HEADS * HEAD_PAD

C_WIDTH = 256
C_GROUP = 16
C_GROUPS = C_WIDTH // C_GROUP
C_STATE = 64
C_STATES = C_GROUPS * C_STATE

M_HEADS = 4
M_HEAD_DIM = 64
M_WIDTH = M_HEADS * M_HEAD_DIM

COL_CKV = 0
COL_KPE = 256
COL_CQ = 384
COL_AV = 1152
COL_AU = 1408
COL_AG = 1664
COL_BG = 1920
COL_CI = 2432
COL_CG = 2688
COL_MQ = 2944
COL_MG = 3200
N_SMALL = 3456

V7X_VMEM_LIMIT_BYTES = 56 * 1024 * 1024

TOKEN_TILE = 256
SUB_TILES = 2
ATTN_TILE = 256
S5_CHUNK = 64
S5_LANE_BLOCK = 256


def _sigmoid(x):
    return 1.0 / (1.0 + jnp.exp(-x))


def _silu(x):
    return x * _sigmoid(x)


def _rms(x, g):
    return x * lax.rsqrt(jnp.mean(x * x, axis=-1, keepdims=True) + EPS) * g


def _dot(a, b):
    return jnp.dot(a, b, preferred_element_type=F32)


def _dot_nt(a, b):
    return lax.dot_general(a, b, (((1,), (1,)), ((), ())), preferred_element_type=F32)


def _layer_spec(a, layer, **kwargs):
    zeros = (0,) * (a.ndim - 1)
    return pl.BlockSpec((None,) + a.shape[1:], lambda *_: (layer,) + zeros, **kwargs)


ROPE_HALF = B_ROPE // 2
ROPE_LO = 0
ROPE_HI = HEAD_PAD // 2


def _head_layout(w):
    nope, rope = w[..., :B_NOPE], w[..., B_NOPE:]
    split = ROPE_HI - ROPE_HALF
    pad = jnp.zeros(w.shape[:-1] + (HEAD_PAD - B_QK_DIM,), w.dtype)
    return jnp.concatenate([rope[..., :ROPE_HALF], nope[..., :split], rope[..., ROPE_HALF:],
                            nope[..., split:], pad], axis=-1)


def _rope_table_body(pos_col_ref, pos_row_ref, freq_row_ref, freq_col_ref,
                     c_ref, s_ref, ct_ref, st_ref):
    ang = pos_col_ref[...] * freq_row_ref[...]
    lane = lax.broadcasted_iota(jnp.int32, ang.shape, 1)
    lo = lane < ROPE_LO + ROPE_HALF
    hi = (lane >= ROPE_HI) & (lane < ROPE_HI + ROPE_HALF)
    cos = jnp.cos(ang)
    sin = jnp.sin(ang)
    c_ref[...] = jnp.where(lo | hi, cos, 1.0)
    s_ref[...] = jnp.where(lo, -sin, jnp.where(hi, sin, 0.0))
    ang_t = freq_col_ref[...] * pos_row_ref[...]
    ct_ref[...] = jnp.cos(ang_t)
    st_ref[...] = jnp.sin(ang_t)


def _rope_tables(positions):
    n_tok = positions.size
    inv_freq = ROPE_THETA ** (-jnp.arange(ROPE_HALF, dtype=F32) / ROPE_HALF)
    freq_row = _head_layout(jnp.concatenate([jnp.zeros((B_NOPE,), F32), inv_freq, inv_freq]))
    tm = 1024
    freq_col = jnp.broadcast_to(inv_freq[:, None], (ROPE_HALF, tm))
    pos = positions.astype(F32)
    tab = jax.ShapeDtypeStruct((n_tok, HEAD_PAD), F32)
    tab_t = jax.ShapeDtypeStruct((ROPE_HALF, n_tok), F32)
    return pl.pallas_call(
        _rope_table_body,
        out_shape=(tab, tab, tab_t, tab_t),
        grid=(n_tok // tm,),
        in_specs=[pl.BlockSpec((tm, 1), lambda i: (i, 0)),
                  pl.BlockSpec((1, tm), lambda i: (0, i)),
                  pl.BlockSpec((1, HEAD_PAD), lambda i: (0, 0)),
                  pl.BlockSpec((ROPE_HALF, tm), lambda i: (0, 0))],
        out_specs=(pl.BlockSpec((tm, HEAD_PAD), lambda i: (i, 0)),) * 2
        + (pl.BlockSpec((ROPE_HALF, tm), lambda i: (0, i)),) * 2,
        compiler_params=pltpu.CompilerParams(dimension_semantics=("parallel",)),
        name="rope_tables",
    )(pos.reshape(n_tok, 1), pos.reshape(1, n_tok), freq_row.reshape(1, HEAD_PAD), freq_col)


def _inproj_body(x_ref, ng_ref, w_ref, ang_ref, wst_ref, bias_ref, qng_ref, kvng_ref,
                 wuqt_ref, wuk_ref, wuvt_ref, gq_ref, gk_ref, c_ref, s_ref, ct_ref, st_ref,
                 ya_ref, qt_ref, k_ref, vt_ref, bg_ref, ci_ref, cg_ref, mq_ref, mg_ref):
    tm = x_ref.shape[0]
    h = _rms(x_ref[...], ng_ref[...]).astype(BF16)

    z_lat = _dot(h, w_ref[:, :COL_AV])

    def lat(lo, width):
        return z_lat[:, lo:lo + width]

    ckv = _rms(lat(COL_CKV, B_KV_RANK), kvng_ref[...]).astype(BF16)
    kf = _dot(ckv, wuk_ref[...])
    z_rest = _dot(h, w_ref[:, COL_AV:])

    def proj(lo, width):
        return z_rest[:, lo - COL_AV:lo - COL_AV + width]

    kpe = lat(COL_KPE, HEAD_PAD)
    c_tab, s_tab = c_ref[...], s_ref[...]
    for hd in range(B_HEADS):
        lanes = slice(hd * HEAD_PAD, (hd + 1) * HEAD_PAD)
        kh = kf[:, lanes] + kpe
        ssq = jnp.sum(kh * kh, axis=-1, keepdims=True)
        kh = kh * lax.rsqrt(ssq * (1.0 / B_QK_DIM) + EPS) * gk_ref[...]
        k_ref[:, lanes] = (kh * c_tab + pltpu.roll(kh, HEAD_PAD // 2, 1) * s_tab).astype(BF16)
    vt_ref[...] = _dot_nt(wuvt_ref[...], ckv).astype(BF16)

    u = jax.nn.gelu(proj(COL_AU, A_WIDTH))
    v = _rms(jax.nn.gelu(proj(COL_AV, A_WIDTH)), ang_ref[...]).astype(BF16)
    ag = _silu(proj(COL_AG, A_WIDTH))
    lane_group = lax.broadcasted_iota(jnp.int32, (A_CHUNK, A_WIDTH), 1) // A_GROUP_DIM
    for c in range(tm // A_CHUNK):
        rows = slice(c * A_CHUNK, (c + 1) * A_CHUNK)
        s_all = _dot(wst_ref[...], v[rows, :])
        s = s_all[(A_GROUPS - 1) * A_CHUNK:, :]
        for g in range(A_GROUPS - 2, -1, -1):
            s = jnp.where(lane_group == g, s_all[g * A_CHUNK:(g + 1) * A_CHUNK, :], s)
        s = s + bias_ref[...]
        ya_ref[rows, :] = (u[rows, :] * s * ag[rows, :]).astype(BF16)

    q_scale = B_QK_DIM ** -0.5 * math.log2(math.e)
    cq = _rms(lat(COL_CQ, B_Q_RANK), qng_ref[...]).astype(BF16)
    half_rows = QK_PAD // 2
    qt_halves = [_dot_nt(wuqt_ref[i * half_rows:(i + 1) * half_rows, :], cq) for i in range(2)]
    cos_t, sin_t = ct_ref[...], st_ref[...]
    for hd in range(B_HEADS):
        r0 = hd * HEAD_PAD
        qr = r0 % half_rows
        qh = qt_halves[r0 // half_rows][qr:qr + HEAD_PAD, :]
        ssq = jnp.sum(qh * qh, axis=0, keepdims=True)
        qh = qh * (lax.rsqrt(ssq * (1.0 / B_QK_DIM) + EPS) * q_scale) * gq_ref[...]
        x1 = qh[ROPE_LO:ROPE_LO + ROPE_HALF, :]
        x2 = qh[ROPE_HI:ROPE_HI + ROPE_HALF, :]
        qt_ref[r0:r0 + HEAD_PAD, :] = qh.astype(BF16)
        qt_ref[r0 + ROPE_LO:r0 + ROPE_LO + ROPE_HALF, :] = (x1 * cos_t - x2 * sin_t).astype(BF16)
        qt_ref[r0 + ROPE_HI:r0 + ROPE_HI + ROPE_HALF, :] = (x2 * cos_t + x1 * sin_t).astype(BF16)

    bg_ref[...] = _silu(proj(COL_BG, B_WIDTH))
    ci_ref[...] = proj(COL_CI, C_WIDTH)
    cg_ref[...] = _silu(proj(COL_CG, C_WIDTH))
    mq_ref[...] = proj(COL_MQ, M_WIDTH)
    mg_ref[...] = _silu(proj(COL_MG, M_WIDTH))


def _inproj_step(x_ref, *refs):
    weights = refs[:12]
    c_ref, s_ref, ct_ref, st_ref = refs[12:16]
    ya_ref, qt_ref, k_ref, vt_ref, bg_ref, ci_ref, cg_ref, mq_ref, mg_ref = refs[16:]
    tm = TOKEN_TILE
    for sub in range(SUB_TILES):
        rows = pl.ds(sub * tm, tm)
        _inproj_body(x_ref.at[rows], *weights, c_ref.at[rows], s_ref.at[rows],
                     ct_ref.at[:, rows], st_ref.at[:, rows],
                     ya_ref.at[rows], qt_ref.at[sub], k_ref.at[rows], vt_ref.at[sub],
                     bg_ref.at[rows], ci_ref.at[rows], cg_ref.at[rows], mq_ref.at[rows],
                     mg_ref.at[rows])


def _inproj(x2, p, layer, tabs, bsz, seq):
    n_tok = x2.shape[0]
    tm = TOKEN_TILE
    ts = SUB_TILES * tm
    nst = seq // ts

    def tok(i):
        return (i, 0)

    def seq_major(i):
        return (i % nst, i // nst)

    weights = (p["norm_g"], p["w_small"], p["a_norm_g"], p["a_wst"], p["a_bias"],
               p["b_q_norm_g"], p["b_kv_norm_g"], p["b_wuqt"], p["b_wuk"], p["b_wuvt"],
               p["b_gq"], p["b_gk"])
    in_specs = ([pl.BlockSpec((ts, D_MODEL), tok)] + [_layer_spec(w, layer) for w in weights]
                + [pl.BlockSpec((ts, HEAD_PAD), tok)] * 2
                + [pl.BlockSpec((ROPE_HALF, ts), lambda i: (0, i))] * 2)
    out_shape = (
        jax.ShapeDtypeStruct((n_tok, A_WIDTH), BF16),
        jax.ShapeDtypeStruct((n_tok // tm, QK_PAD, tm), BF16),
        jax.ShapeDtypeStruct((n_tok, QK_PAD), BF16),
        jax.ShapeDtypeStruct((n_tok // tm, B_WIDTH, tm), BF16),
        jax.ShapeDtypeStruct((n_tok, B_WIDTH), F32),
        jax.ShapeDtypeStruct((seq, bsz * C_WIDTH), F32),
        jax.ShapeDtypeStruct((seq, bsz * C_WIDTH), F32),
        jax.ShapeDtypeStruct((n_tok, M_WIDTH), F32),
        jax.ShapeDtypeStruct((n_tok, M_WIDTH), F32),
    )
    out_specs = (
        pl.BlockSpec((ts, A_WIDTH), tok),
        pl.BlockSpec((SUB_TILES, QK_PAD, tm), lambda i: (i, 0, 0)),
        pl.BlockSpec((ts, QK_PAD), tok),
        pl.BlockSpec((SUB_TILES, B_WIDTH, tm), lambda i: (i, 0, 0)),
        pl.BlockSpec((ts, B_WIDTH), tok),
        pl.BlockSpec((ts, C_WIDTH), seq_major),
        pl.BlockSpec((ts, C_WIDTH), seq_major),
        pl.BlockSpec((ts, M_WIDTH), tok),
        pl.BlockSpec((ts, M_WIDTH), tok),
    )
    return pl.pallas_call(
        _inproj_step,
        out_shape=out_shape,
        grid=(n_tok // ts,),
        in_specs=in_specs,
        out_specs=out_specs,
        compiler_params=pltpu.CompilerParams(
            dimension_semantics=("parallel",), vmem_limit_bytes=V7X_VMEM_LIMIT_BYTES),
        name="inproj",
    )(x2, *weights, *tabs)


def _attn_body(qt_ref, k_ref, vt_ref, g_ref, o_ref, m_ref, a_ref, l_ref, acc_ref, s_ref):
    tq = qt_ref.shape[1]
    qi = pl.program_id(1)
    key_le_query = (lax.broadcasted_iota(jnp.int32, (tq, tq), 0)
                    <= lax.broadcasted_iota(jnp.int32, (tq, tq), 1))
    m_ref[1] = jnp.full(m_ref.shape[1:], -jnp.inf, F32)
    l_ref[...] = jnp.zeros(l_ref.shape, F32)
    acc_ref[...] = jnp.zeros(acc_ref.shape, F32)

    def scores(j, masked, slot):
        r0 = pl.multiple_of(j * tq, tq)
        for hd in range(B_HEADS):
            lanes = slice(hd * HEAD_PAD, (hd + 1) * HEAD_PAD)
            st = _dot(k_ref[pl.ds(r0, tq), lanes], qt_ref[lanes, :])
            if masked:
                st = jnp.where(key_le_query, st, -jnp.inf)
            s_ref[slot, hd] = st
            m_old = m_ref[1 - slot, hd:hd + 1, :]
            m_new = jnp.maximum(m_old, jnp.max(st, axis=0, keepdims=True))
            m_ref[slot, hd:hd + 1, :] = m_new
            a_ref[slot, hd:hd + 1, :] = jnp.exp2(m_old - m_new)

    def accumulate(j, slot):
        for hd in range(B_HEADS):
            rows = slice(hd * B_VDIM, (hd + 1) * B_VDIM)
            alpha = a_ref[slot, hd:hd + 1, :]
            pt = jnp.exp2(s_ref[slot, hd] - m_ref[slot, hd:hd + 1, :])
            l_ref[hd:hd + 1, :] = alpha * l_ref[hd:hd + 1, :] + jnp.sum(pt, axis=0, keepdims=True)
            acc_ref[rows, :] = (alpha * acc_ref[rows, :]
                                + _dot(vt_ref[j, rows, :], pt.astype(BF16)))

    @pl.when(qi == 0)
    def _():
        scores(0, True, 0)
        accumulate(0, 0)

    @pl.when(qi > 0)
    def _():
        scores(0, False, 0)

        def two_blocks(i, carry):
            scores(2 * i + 1, False, 1)
            accumulate(2 * i, 0)
            scores(2 * i + 2, False, 0)
            accumulate(2 * i + 1, 1)
            return carry

        lax.fori_loop(0, (qi - 1) // 2, two_blocks, 0)

        @pl.when(qi % 2 == 0)
        def _():
            scores(qi - 1, False, 1)
            accumulate(qi - 2, 0)
            scores(qi, True, 0)
            accumulate(qi - 1, 1)
            accumulate(qi, 0)

        @pl.when(qi % 2 == 1)
        def _():
            scores(qi, True, 1)
            accumulate(qi - 1, 0)
            accumulate(qi, 1)

    for hd in range(B_HEADS):
        rows = slice(hd * B_VDIM, (hd + 1) * B_VDIM)
        acc_ref[rows, :] = acc_ref[rows, :] * (1.0 / l_ref[hd:hd + 1, :])
    o_ref[...] = (acc_ref[...].T * g_ref[...]).astype(BF16)


def _attention(q, k, vt, gate, bsz, seq):
    tq = ATTN_TILE
    nq = seq // tq
    return pl.pallas_call(
        _attn_body,
        out_shape=jax.ShapeDtypeStruct((bsz * seq, B_WIDTH), BF16),
        grid=(bsz, nq),
        in_specs=[pl.BlockSpec((None, QK_PAD, tq), lambda b, i: (b * nq + i, 0, 0)),
                  pl.BlockSpec((seq, QK_PAD), lambda b, i: (b, 0)),
                  pl.BlockSpec((nq, B_WIDTH, tq), lambda b, i: (b, 0, 0)),
                  pl.BlockSpec((tq, B_WIDTH), lambda b, i: (b * nq + i, 0))],
        out_specs=pl.BlockSpec((tq, B_WIDTH), lambda b, i: (b * nq + i, 0)),
        scratch_shapes=[pltpu.VMEM((2, B_HEADS, tq), F32), pltpu.VMEM((2, B_HEADS, tq), F32),
                        pltpu.VMEM((B_HEADS, tq), F32), pltpu.VMEM((B_WIDTH, tq), F32),
                        pltpu.VMEM((2, B_HEADS, tq, tq), F32)],
        compiler_params=pltpu.CompilerParams(
            dimension_semantics=("parallel", "parallel"),
            vmem_limit_bytes=V7X_VMEM_LIMIT_BYTES),
        name="latent_attention",
    )(q, k, vt, gate)


def _s5_discretise_body(are_ref, aim_ref, ldt_ref, bre_ref, bim_ref,
                        abar_re_ref, abar_im_ref, bbar_re_ref, bbar_im_ref):
    a_re, a_im = are_ref[...], aim_ref[...]
    dt = jnp.exp(ldt_ref[...])
    mag = jnp.exp(a_re * dt)
    abar_re = mag * jnp.cos(a_im * dt)
    abar_im = mag * jnp.sin(a_im * dt)
    num_re = abar_re - 1.0
    inv_den = 1.0 / (a_re * a_re + a_im * a_im)
    q_re = (num_re * a_re + abar_im * a_im) * inv_den
    q_im = (abar_im * a_re - num_re * a_im) * inv_den
    b_re, b_im = bre_ref[...], bim_ref[...]
    abar_re_ref[...] = abar_re
    abar_im_ref[...] = abar_im
    bbar_re_ref[...] = q_re * b_re - q_im * b_im
    bbar_im_ref[...] = q_re * b_im + q_im * b_re


def _s5_discretise(a_re, a_im, log_dt, b_re, b_im):
    out = jax.ShapeDtypeStruct(a_re.shape, F32)
    return pl.pallas_call(_s5_discretise_body, out_shape=(out,) * 4, name="s5_discretise")(
        a_re, a_im, log_dt, b_re, b_im)


def _s5_body(u_ref, cg_ref, bmat_ref, cmat_ref, are_ref, aim_ref, d_ref, wglu_ref, bglu_ref,
             y_ref, state_ref, buf_ref):
    lt, bsz, width = u_ref.shape
    rows = lt * bsz
    lb_w = S5_LANE_BLOCK

    @pl.when(pl.program_id(0) == 0)
    def _():
        state_ref[...] = jnp.zeros_like(state_ref)

    u2 = u_ref[...].reshape(rows, width)
    ub = u2.astype(BF16)
    half = rows // 2
    y_halves = [None, None]
    for lb in range(C_STATES // lb_w):
        cols = slice(2 * lb * lb_w, 2 * (lb + 1) * lb_w)
        re_l = slice(2 * lb * lb_w, (2 * lb + 1) * lb_w)
        im_l = slice((2 * lb + 1) * lb_w, 2 * (lb + 1) * lb_w)
        buf_ref[:, cols] = _dot(ub, bmat_ref[:, cols])
        a_re = jnp.broadcast_to(are_ref[:, lb * lb_w:(lb + 1) * lb_w], (bsz, lb_w))
        a_im = jnp.broadcast_to(aim_ref[:, lb * lb_w:(lb + 1) * lb_w], (bsz, lb_w))
        s_re, s_im = state_ref[:, re_l], state_ref[:, im_l]
        for t in range(lt):
            r = slice(t * bsz, (t + 1) * bsz)
            n_re = a_re * s_re - a_im * s_im + buf_ref[r, re_l]
            n_im = a_re * s_im + a_im * s_re + buf_ref[r, im_l]
            buf_ref[r, re_l] = n_re
            buf_ref[r, im_l] = n_im
            s_re, s_im = n_re, n_im
        state_ref[:, re_l] = s_re
        state_ref[:, im_l] = s_im
        for hf in range(2):
            r = slice(hf * half, (hf + 1) * half)
            part = _dot(buf_ref[r, cols].astype(BF16), cmat_ref[cols, :])
            y_halves[hf] = part if y_halves[hf] is None else y_halves[hf] + part

    y = jnp.concatenate(y_halves, axis=0) + d_ref[...] * u2
    y = jax.nn.gelu(y)
    y = y * _sigmoid(_dot(y.astype(BF16), wglu_ref[...]) + bglu_ref[...])
    y_ref[...] = (y * cg_ref[...].reshape(rows, width)).reshape(lt, bsz, width)


def _s5(c_in, c_gate, p, layer, bsz, seq):
    lt = S5_CHUNK
    u3 = c_in.reshape(seq, bsz, C_WIDTH)
    g3 = c_gate.reshape(seq, bsz, C_WIDTH)

    def chunk(i):
        return (i, 0, 0)

    weights = (p["c_bmat"], p["c_cmat"], p["c_are"], p["c_aim"], p["c_d"], p["c_w_glu"],
               p["c_b_glu"])
    y = pl.pallas_call(
        _s5_body,
        out_shape=jax.ShapeDtypeStruct((seq, bsz, C_WIDTH), F32),
        grid=(seq // lt,),
        in_specs=[pl.BlockSpec((lt, bsz, C_WIDTH), chunk)] * 2
        + [_layer_spec(w, layer) for w in weights],
        out_specs=pl.BlockSpec((lt, bsz, C_WIDTH), chunk),
        scratch_shapes=[pltpu.VMEM((bsz, 2 * C_STATES), F32),
                        pltpu.VMEM((lt * bsz, 2 * C_STATES), F32)],
        compiler_params=pltpu.CompilerParams(
            dimension_semantics=("arbitrary",), vmem_limit_bytes=V7X_VMEM_LIMIT_BYTES),
        name="s5_scan",
    )(u3, g3, *weights)
    return y.reshape(seq, bsz * C_WIDTH)


def _head_rms(x, gain, n_heads, head_dim):
    head_id = lax.broadcasted_iota(jnp.int32, x.shape, 1) // head_dim
    out = jnp.zeros_like(x)
    for hd in range(n_heads):
        xh = jnp.where(head_id == hd, x, 0.0)
        ssq = jnp.sum(xh * xh, axis=-1, keepdims=True)
        out = out + xh * lax.rsqrt(ssq * (1.0 / head_dim) + EPS)
    return out * gain


def _memkv_body(mem_ref, ng_ref, wkv_ref, gk_ref, k_ref, v_ref):
    mem_len = mem_ref.shape[0]
    mh = _rms(mem_ref[...], ng_ref[...]).astype(BF16)
    kv = _dot(mh, wkv_ref[...])
    k = _head_rms(kv[:, :M_WIDTH], gk_ref[...], M_HEADS, M_HEAD_DIM)
    v = kv[:, M_WIDTH:]
    head_id = lax.broadcasted_iota(jnp.int32, k.shape, 1) // M_HEAD_DIM
    for hd in range(M_HEADS):
        rows = slice(hd * mem_len, (hd + 1) * mem_len)
        k_ref[rows, :] = jnp.where(head_id == hd, k, 0.0).astype(BF16)
        v_ref[rows, :] = jnp.where(head_id == hd, v, 0.0).astype(BF16)


def _memkv(mem2, p, layer, bsz, mem_len):
    weights = (p["m_norm_g"], p["m_w_kv"], p["m_gk"])
    out = jax.ShapeDtypeStruct((bsz * M_HEADS * mem_len, M_WIDTH), BF16)
    return pl.pallas_call(
        _memkv_body,
        out_shape=(out, out),
        grid=(bsz,),
        in_specs=[pl.BlockSpec((mem_len, D_MODEL), lambda b: (b, 0))]
        + [_layer_spec(w, layer) for w in weights],
        out_specs=(pl.BlockSpec((M_HEADS * mem_len, M_WIDTH), lambda b: (b, 0)),) * 2,
        compiler_params=pltpu.CompilerParams(dimension_semantics=("parallel",)),
        name="memory_kv",
    )(mem2, *weights)


def _merge_body(x_ref, ng_ref, wm_ref, bm_ref, ya_ref, yb_ref, yc_ref, mq_ref, mg_ref,
                mk_ref, mv_ref, gq_ref, wa_ref, wb_ref, wc_ref, wmm_ref, wo_ref, o_ref):
    x = x_ref[...]
    h = _rms(x, ng_ref[...]).astype(BF16)

    def gated(br, y, w_ref):
        cols = slice(br * D_MODEL, (br + 1) * D_MODEL)
        gate = _sigmoid(_dot(h, wm_ref[:, cols]) + bm_ref[:, cols])
        return gate * _dot(y, w_ref[...])

    mq = mq_ref[...]
    mem_len = mk_ref.shape[0] // M_HEADS
    same_head = (lax.broadcasted_iota(jnp.int32, (M_WIDTH, M_WIDTH), 0) // M_HEAD_DIM
                 == lax.broadcasted_iota(jnp.int32, (M_WIDTH, M_WIDTH), 1) // M_HEAD_DIM)
    head_ones = jnp.where(same_head, 1.0, 0.0).astype(BF16)
    q_sq = mq * mq
    q_sq_hi = q_sq.astype(BF16)
    q_sq_lo = (q_sq - q_sq_hi.astype(F32)).astype(BF16)
    ssq = _dot(q_sq_hi, head_ones) + _dot(q_sq_lo, head_ones)
    merged = gated(0, ya_ref[...], wa_ref)
    qn = (mq * lax.rsqrt(ssq * (1.0 / M_HEAD_DIM) + EPS) * gq_ref[...]).astype(BF16)
    s_all = _dot_nt(qn, mk_ref[...])
    merged = merged + gated(1, yb_ref[...], wb_ref)
    probs = []
    for hd in range(M_HEADS):
        s = s_all[:, hd * mem_len:(hd + 1) * mem_len]
        e = jnp.exp(s - jnp.max(s, axis=-1, keepdims=True))
        probs.append((e * (1.0 / jnp.sum(e, axis=-1, keepdims=True))).astype(BF16))
    om = _dot(jnp.concatenate(probs, axis=1), mv_ref[...])
    merged = merged + gated(2, yc_ref[...].astype(BF16), wc_ref)
    ym = (om * mg_ref[...]).astype(BF16)
    merged = merged + gated(3, ym, wmm_ref)
    o_ref[...] = x + _dot(merged.astype(BF16), wo_ref[...])


def _merge_step(x_ref, ng_ref, wm_ref, bm_ref, ya_ref, yb_ref, yc_ref, mq_ref, mg_ref,
                mk_ref, mv_ref, gq_ref, wa_ref, wb_ref, wc_ref, wmm_ref, wo_ref, o_ref):
    tm = TOKEN_TILE
    for sub in range(SUB_TILES):
        rows = pl.ds(sub * tm, tm)
        _merge_body(x_ref.at[rows], ng_ref, wm_ref, bm_ref, ya_ref.at[rows], yb_ref.at[rows],
                    yc_ref.at[rows], mq_ref.at[rows], mg_ref.at[rows], mk_ref, mv_ref, gq_ref,
                    wa_ref, wb_ref, wc_ref, wmm_ref, wo_ref, o_ref.at[rows])


def _merge(x2, ya, yb, yc, mq, mg, mk, mv, p, layer, bsz, seq, mem_len):
    n_tok = x2.shape[0]
    ts = SUB_TILES * TOKEN_TILE
    nst = seq // ts

    def tok(i):
        return (i, 0)

    def seq_major(i):
        return (i % nst, i // nst)

    def per_batch(i):
        return (i // nst, 0)

    def resident(a):
        return _layer_spec(a, layer, pipeline_mode=pl.Buffered(1))

    in_specs = [
        pl.BlockSpec((ts, D_MODEL), tok),
        resident(p["norm_g"]), resident(p["w_merge"]), resident(p["b_merge"]),
        pl.BlockSpec((ts, A_WIDTH), tok),
        pl.BlockSpec((ts, B_WIDTH), tok),
        pl.BlockSpec((ts, C_WIDTH), seq_major),
        pl.BlockSpec((ts, M_WIDTH), tok),
        pl.BlockSpec((ts, M_WIDTH), tok),
        pl.BlockSpec((M_HEADS * mem_len, M_WIDTH), per_batch),
        pl.BlockSpec((M_HEADS * mem_len, M_WIDTH), per_batch),
        resident(p["m_gq"]), resident(p["w_br_a"]), resident(p["w_br_b"]),
        resident(p["w_br_c"]), resident(p["w_br_m"]), resident(p["w_out"]),
    ]
    return pl.pallas_call(
        _merge_step,
        out_shape=jax.ShapeDtypeStruct((n_tok, D_MODEL), F32),
        grid=(n_tok // ts,),
        in_specs=in_specs,
        out_specs=pl.BlockSpec((ts, D_MODEL), tok),
        compiler_params=pltpu.CompilerParams(
            dimension_semantics=("parallel",), vmem_limit_bytes=V7X_VMEM_LIMIT_BYTES),
        name="merge",
    )(x2, p["norm_g"], p["w_merge"], p["b_merge"], ya, yb, yc, mq, mg, mk, mv,
      p["m_gq"], p["w_br_a"], p["w_br_b"], p["w_br_c"], p["w_br_m"], p["w_out"])


def _row(v):
    return v.reshape(1, -1).astype(F32)


def _prep_layer(norm_g, w_in, b_merge, a_norm_g, a_w_s, a_b_s, b_q_norm_g, b_kv_norm_g,
                b_w_uq, b_w_ukv, b_qk_g_q, b_qk_g_k, c_a_re, c_a_im, c_log_dt, c_b_re,
                c_b_im, c_c_re, c_c_im, c_d, c_w_glu, c_b_glu, m_norm_g, m_w_kv, m_qk_g_q,
                m_qk_g_k, w_br_a, w_br_b, w_br_c, w_br_m, w_out):
    p = {}
    p["norm_g"] = _row(norm_g)
    kpe = _head_layout(jnp.pad(w_in[:, 1792:1824], ((0, 0), (B_NOPE, 0))))
    p["w_small"] = jnp.concatenate(
        [w_in[:, 1536:1792], kpe, w_in[:, 768:1536], w_in[:, 256:512], w_in[:, :256],
         w_in[:, 512:768], w_in[:, 1824:3360]], axis=1).astype(BF16)
    p["w_merge"] = w_in[:, 3360:].astype(BF16)
    p["b_merge"] = _row(b_merge)

    p["a_norm_g"] = _row(a_norm_g)
    causal = jnp.tril(jnp.ones((A_CHUNK, A_CHUNK), dtype=bool))
    p["a_wst"] = jnp.where(causal, a_w_s, 0.0).reshape(A_GROUPS * A_CHUNK, A_CHUNK).astype(BF16)
    p["a_bias"] = jnp.repeat(a_b_s.T, A_GROUP_DIM, axis=1).astype(F32)

    p["b_q_norm_g"] = _row(b_q_norm_g)
    p["b_kv_norm_g"] = _row(b_kv_norm_g)
    wuq = _head_layout(b_w_uq.reshape(B_Q_RANK, B_HEADS, B_QK_DIM))
    p["b_wuqt"] = wuq.reshape(B_Q_RANK, QK_PAD).T.astype(BF16)
    wukv = b_w_ukv.reshape(B_KV_RANK, B_HEADS, B_NOPE + B_VDIM)
    wuk = _head_layout(jnp.pad(wukv[:, :, :B_NOPE], ((0, 0), (0, 0), (0, B_ROPE))))
    p["b_wuk"] = wuk.reshape(B_KV_RANK, QK_PAD).astype(BF16)
    p["b_wuvt"] = wukv[:, :, B_NOPE:].reshape(B_KV_RANK, B_WIDTH).T.astype(BF16)
    p["b_gq"] = jnp.broadcast_to(_head_layout(b_qk_g_q.astype(F32))[:, None],
                                 (HEAD_PAD, TOKEN_TILE))
    p["b_gk"] = _row(_head_layout(b_qk_g_k))

    def rep(a):
        return jnp.repeat(a.astype(F32), C_GROUP, axis=0)

    def gcp(b):
        return b.astype(F32).transpose(0, 2, 1).reshape(C_WIDTH, C_STATE)

    log_dt = jnp.broadcast_to(c_log_dt.astype(F32)[:, None], (C_GROUPS, C_STATE))
    abar_re, abar_im, bbar_re, bbar_im = _s5_discretise(
        rep(c_a_re), rep(c_a_im), rep(log_dt), gcp(c_b_re), gcp(c_b_im))
    eye = jnp.eye(C_GROUPS, dtype=F32)

    def expand_in(m):
        m = m.reshape(C_GROUPS, C_GROUP, C_STATE)
        return (eye[:, None, :, None] * m[:, :, None, :]).reshape(C_WIDTH, C_STATES)

    def expand_out(m):
        return (eye[:, None, :, None] * m.transpose(0, 2, 1)[:, :, None, :]).reshape(
            C_STATES, C_WIDTH)

    n_lb = C_STATES // S5_LANE_BLOCK
    bmat = jnp.stack([expand_in(bbar_re).reshape(C_WIDTH, n_lb, S5_LANE_BLOCK),
                      expand_in(bbar_im).reshape(C_WIDTH, n_lb, S5_LANE_BLOCK)], axis=2)
    p["c_bmat"] = bmat.reshape(C_WIDTH, 2 * C_STATES).astype(BF16)
    cmat = jnp.stack([expand_out(c_c_re.astype(F32)).reshape(n_lb, S5_LANE_BLOCK, C_WIDTH),
                      -expand_out(c_c_im.astype(F32)).reshape(n_lb, S5_LANE_BLOCK, C_WIDTH)],
                     axis=1)
    p["c_cmat"] = cmat.reshape(2 * C_STATES, C_WIDTH).astype(BF16)
    p["c_are"] = _row(abar_re.reshape(C_GROUPS, C_GROUP, C_STATE)[:, 0, :])
    p["c_aim"] = _row(abar_im.reshape(C_GROUPS, C_GROUP, C_STATE)[:, 0, :])
    p["c_d"] = _row(c_d)
    p["c_w_glu"] = c_w_glu.astype(BF16)
    p["c_b_glu"] = _row(c_b_glu)

    p["m_norm_g"] = _row(m_norm_g)
    p["m_w_kv"] = m_w_kv.astype(BF16)
    p["m_gq"] = _row(jnp.tile(m_qk_g_q, M_HEADS)) * (M_HEAD_DIM ** -0.5)
    p["m_gk"] = _row(jnp.tile(m_qk_g_k, M_HEADS))
    p["w_br_a"] = w_br_a.astype(BF16)
    p["w_br_b"] = w_br_b.astype(BF16)
    p["w_br_c"] = w_br_c.astype(BF16)
    p["w_br_m"] = w_br_m.astype(BF16)
    p["w_out"] = w_out.astype(BF16)
    return p


def kernel(x, mem, positions, norm_g, w_in, b_merge, a_norm_g, a_w_s, a_b_s, b_q_norm_g, b_kv_norm_g, b_w_uq, b_w_ukv, b_qk_g_q, b_qk_g_k, c_a_re, c_a_im, c_log_dt, c_b_re, c_b_im, c_c_re, c_c_im, c_d, c_w_glu, c_b_glu, m_norm_g, m_w_kv, m_qk_g_q, m_qk_g_k, w_br_a, w_br_b, w_br_c, w_br_m, w_out):
    bsz, seq, d_model = x.shape
    mem_len = mem.shape[1]
    depth = norm_g.shape[0]
    assert d_model == D_MODEL and seq % (SUB_TILES * TOKEN_TILE) == 0
    assert seq % S5_CHUNK == 0 and bsz == 8 and TOKEN_TILE == ATTN_TILE

    stacked = (norm_g, w_in, b_merge, a_norm_g, a_w_s, a_b_s, b_q_norm_g, b_kv_norm_g,
               b_w_uq, b_w_ukv, b_qk_g_q, b_qk_g_k, c_a_re, c_a_im, c_log_dt, c_b_re,
               c_b_im, c_c_re, c_c_im, c_d, c_w_glu, c_b_glu, m_norm_g, m_w_kv, m_qk_g_q,
               m_qk_g_k, w_br_a, w_br_b, w_br_c, w_br_m, w_out)

    tabs = _rope_tables(positions)
    x2 = x.reshape(bsz * seq, d_model)
    mem2 = mem.reshape(bsz * mem_len, d_model)
    p = jax.vmap(_prep_layer)(*stacked)
    for layer in range(depth):
        ya, q, k, vt, bg, ci, cg, mq, mg = _inproj(x2, p, layer, tabs, bsz, seq)
        yb = _attention(q, k, vt, bg, bsz, seq)
        yc = _s5(ci, cg, p, layer, bsz, seq)
        mk, mv = _memkv(mem2, p, layer, bsz, mem_len)
        x2 = _merge(x2, ya, yb, yc, mq, mg, mk, mv, p, layer, bsz, seq, mem_len)
    return x2.reshape(bsz, seq, d_model)
```

```python
import functools
import math

import jax
import jax.numpy as jnp
from jax import lax
from jax.experimental import pallas as pl
from jax.experimental.pallas import tpu as pltpu

F32 = jnp.float32
BF16 = jnp.bfloat16

D_MODEL = 1024
EPS = 1e-6
N_BRANCH = 4

A_WIDTH = 256
A_GROUPS = 4
A_GROUP_DIM = A_WIDTH // A_GROUPS
A_CHUNK = 128

B_HEADS = 8
B_NOPE = 64
B_ROPE = 32
B_QK_DIM = B_NOPE + B_ROPE
B_VDIM = 64
B_Q_RANK = 768
B_KV_RANK = 256
B_WIDTH = B_HEADS * B_VDIM
ROPE_THETA = 10000.0
HEAD_PAD = 128
QK_PAD = B_HEADS * HEAD_PAD
V_AUG = B_VDIM + 16
VT_ROWS = B_HEADS * V_AUG

C_WIDTH = 256
C_GROUP = 16
C_GROUPS = C_WIDTH // C_GROUP
C_STATE = 64
C_STATES = C_GROUPS * C_STATE

M_HEADS = 4
M_HEAD_DIM = 64
M_WIDTH = M_HEADS * M_HEAD_DIM

COL_CKV = 0
COL_KPE = 256
COL_CQ = 384
COL_AV = 1152
COL_AU = 1408
COL_AG = 1664
COL_BG = 1920
COL_CI = 2432
COL_CG = 2688
COL_MQ = 2944
COL_MG = 3200
N_SMALL = 3456

V7X_VMEM_LIMIT_BYTES = 56 * 1024 * 1024

TOKEN_TILE = 256
SUB_TILES = 2
ATTN_TILE = 256
S5_CHUNK = 64
S5_LANE_BLOCK = 256


def _sigmoid(x):
    return 1.0 / (1.0 + jnp.exp(-x))


def _silu(x):
    return x * _sigmoid(x)


def _rms(x, g):
    return x * lax.rsqrt(jnp.mean(x * x, axis=-1, keepdims=True) + EPS) * g


def _dot(a, b):
    return jnp.dot(a, b, preferred_element_type=F32)


def _dot_nt(a, b):
    return lax.dot_general(a, b, (((1,), (1,)), ((), ())), preferred_element_type=F32)


def _layer_spec(a, layer, **kwargs):
    zeros = (0,) * (a.ndim - 1)
    return pl.BlockSpec((None,) + a.shape[1:], lambda *_: (layer,) + zeros, **kwargs)


ROPE_HALF = B_ROPE // 2
ROPE_LO = 0
ROPE_HI = HEAD_PAD // 2


def _head_layout(w):
    nope, rope = w[..., :B_NOPE], w[..., B_NOPE:]
    split = ROPE_HI - ROPE_HALF
    pad = jnp.zeros(w.shape[:-1] + (HEAD_PAD - B_QK_DIM,), w.dtype)
    return jnp.concatenate([rope[..., :ROPE_HALF], nope[..., :split], rope[..., ROPE_HALF:],
                            nope[..., split:], pad], axis=-1)


def _rope_table_body(pos_col_ref, pos_row_ref, freq_row_ref, freq_col_ref,
                     c_ref, s_ref, ct_ref, st_ref):
    ang = pos_col_ref[...] * freq_row_ref[...]
    lane = lax.broadcasted_iota(jnp.int32, ang.shape, 1)
    lo = lane < ROPE_LO + ROPE_HALF
    hi = (lane >= ROPE_HI) & (lane < ROPE_HI + ROPE_HALF)
    cos = jnp.cos(ang)
    sin = jnp.sin(ang)
    c_ref[...] = jnp.where(lo | hi, cos, 1.0)
    s_ref[...] = jnp.where(lo, -sin, jnp.where(hi, sin, 0.0))
    ang_t = freq_col_ref[...] * pos_row_ref[...]
    ct_ref[...] = jnp.cos(ang_t)
    st_ref[...] = jnp.sin(ang_t)


def _rope_tables(positions):
    n_tok = positions.size
    inv_freq = ROPE_THETA ** (-jnp.arange(ROPE_HALF, dtype=F32) / ROPE_HALF)
    freq_row = _head_layout(jnp.concatenate([jnp.zeros((B_NOPE,), F32), inv_freq, inv_freq]))
    tm = 1024
    freq_col = jnp.broadcast_to(inv_freq[:, None], (ROPE_HALF, tm))
    pos = positions.astype(F32)
    tab = jax.ShapeDtypeStruct((n_tok, HEAD_PAD), F32)
    tab_t = jax.ShapeDtypeStruct((ROPE_HALF, n_tok), F32)
    return pl.pallas_call(
        _rope_table_body,
        out_shape=(tab, tab, tab_t, tab_t),
        grid=(n_tok // tm,),
        in_specs=[pl.BlockSpec((tm, 1), lambda i: (i, 0)),
                  pl.BlockSpec((1, tm), lambda i: (0, i)),
                  pl.BlockSpec((1, HEAD_PAD), lambda i: (0, 0)),
                  pl.BlockSpec((ROPE_HALF, tm), lambda i: (0, 0))],
        out_specs=(pl.BlockSpec((tm, HEAD_PAD), lambda i: (i, 0)),) * 2
        + (pl.BlockSpec((ROPE_HALF, tm), lambda i: (0, i)),) * 2,
        compiler_params=pltpu.CompilerParams(dimension_semantics=("parallel",)),
        name="rope_tables",
    )(pos.reshape(n_tok, 1), pos.reshape(1, n_tok), freq_row.reshape(1, HEAD_PAD), freq_col)


def _inproj_body(x_ref, ng_ref, w_ref, ang_ref, wst_ref, bias_ref, qng_ref, kvng_ref,
                 wuqt_ref, wuk_ref, wuvt_ref, gq_ref, gk_ref, c_ref, s_ref, ct_ref, st_ref,
                 ya_ref, qt_ref, k_ref, vt_ref, bg_ref, ci_ref, cg_ref, mq_ref, mg_ref):
    tm = x_ref.shape[0]
    h = _rms(x_ref[...], ng_ref[...]).astype(BF16)

    z_lat = _dot(h, w_ref[:, :COL_AV])

    def lat(lo, width):
        return z_lat[:, lo:lo + width]

    ckv = _rms(lat(COL_CKV, B_KV_RANK), kvng_ref[...]).astype(BF16)
    kf = _dot(ckv, wuk_ref[...])
    z_rest = _dot(h, w_ref[:, COL_AV:])

    def proj(lo, width):
        return z_rest[:, lo - COL_AV:lo - COL_AV + width]

    kpe = lat(COL_KPE, HEAD_PAD)
    c_tab, s_tab = c_ref[...], s_ref[...]
    for hd in range(B_HEADS):
        lanes = slice(hd * HEAD_PAD, (hd + 1) * HEAD_PAD)
        kh = kf[:, lanes] + kpe
        ssq = jnp.sum(kh * kh, axis=-1, keepdims=True)
        kh = kh * lax.rsqrt(ssq * (1.0 / B_QK_DIM) + EPS) * gk_ref[...]
        k_ref[:, lanes] = (kh * c_tab + pltpu.roll(kh, HEAD_PAD // 2, 1) * s_tab).astype(BF16)
    vt = _dot_nt(wuvt_ref[...], ckv)
    ones_row = lax.broadcasted_iota(jnp.int32, vt.shape, 0) % V_AUG == B_VDIM
    vt_ref[...] = jnp.where(ones_row, 1.0, vt).astype(BF16)

    u = jax.nn.gelu(proj(COL_AU, A_WIDTH))
    v = _rms(jax.nn.gelu(proj(COL_AV, A_WIDTH)), ang_ref[...]).astype(BF16)
    ag = _silu(proj(COL_AG, A_WIDTH))
    lane_group = lax.broadcasted_iota(jnp.int32, (A_CHUNK, A_WIDTH), 1) // A_GROUP_DIM
    for c in range(tm // A_CHUNK):
        rows = slice(c * A_CHUNK, (c + 1) * A_CHUNK)
        s_all = _dot(wst_ref[...], v[rows, :])
        s = s_all[(A_GROUPS - 1) * A_CHUNK:, :]
        for g in range(A_GROUPS - 2, -1, -1):
            s = jnp.where(lane_group == g, s_all[g * A_CHUNK:(g + 1) * A_CHUNK, :], s)
        s = s + bias_ref[...]
        ya_ref[rows, :] = (u[rows, :] * s * ag[rows, :]).astype(BF16)

    q_scale = B_QK_DIM ** -0.5 * math.log2(math.e)
    cq = _rms(lat(COL_CQ, B_Q_RANK), qng_ref[...]).astype(BF16)
    half_rows = QK_PAD // 2
    qt_halves = [_dot_nt(wuqt_ref[i * half_rows:(i + 1) * half_rows, :], cq) for i in range(2)]
    cos_t, sin_t = ct_ref[...], st_ref[...]
    for hd in range(B_HEADS):
        r0 = hd * HEAD_PAD
        qr = r0 % half_rows
        qh = qt_halves[r0 // half_rows][qr:qr + HEAD_PAD, :]
        ssq = jnp.sum(qh * qh, axis=0, keepdims=True)
        qh = qh * (lax.rsqrt(ssq * (1.0 / B_QK_DIM) + EPS) * q_scale) * gq_ref[...]
        x1 = qh[ROPE_LO:ROPE_LO + ROPE_HALF, :]
        x2 = qh[ROPE_HI:ROPE_HI + ROPE_HALF, :]
        qt_ref[r0:r0 + HEAD_PAD, :] = qh.astype(BF16)
        qt_ref[r0 + ROPE_LO:r0 + ROPE_LO + ROPE_HALF, :] = (x1 * cos_t - x2 * sin_t).astype(BF16)
        qt_ref[r0 + ROPE_HI:r0 + ROPE_HI + ROPE_HALF, :] = (x2 * cos_t + x1 * sin_t).astype(BF16)

    bg_ref[...] = _silu(proj(COL_BG, B_WIDTH))
    ci_ref[...] = proj(COL_CI, C_WIDTH)
    cg_ref[...] = _silu(proj(COL_CG, C_WIDTH))
    mq_ref[...] = proj(COL_MQ, M_WIDTH)
    mg_ref[...] = _silu(proj(COL_MG, M_WIDTH))


def _inproj_step(x_ref, *refs):
    weights = refs[:12]
    c_ref, s_ref, ct_ref, st_ref = refs[12:16]
    ya_ref, qt_ref, k_ref, vt_ref, bg_ref, ci_ref, cg_ref, mq_ref, mg_ref = refs[16:]
    tm = TOKEN_TILE
    for sub in range(SUB_TILES):
        rows = pl.ds(sub * tm, tm)
        _inproj_body(x_ref.at[rows], *weights, c_ref.at[rows], s_ref.at[rows],
                     ct_ref.at[:, rows], st_ref.at[:, rows],
                     ya_ref.at[rows], qt_ref.at[sub], k_ref.at[rows], vt_ref.at[sub],
                     bg_ref.at[rows], ci_ref.at[rows], cg_ref.at[rows], mq_ref.at[rows],
                     mg_ref.at[rows])


def _inproj(x2, p, layer, tabs, bsz, seq):
    n_tok = x2.shape[0]
    tm = TOKEN_TILE
    ts = SUB_TILES * tm
    nst = seq // ts

    def tok(i):
        return (i, 0)

    def seq_major(i):
        return (i % nst, i // nst)

    weights = (p["norm_g"], p["w_small"], p["a_norm_g"], p["a_wst"], p["a_bias"],
               p["b_q_norm_g"], p["b_kv_norm_g"], p["b_wuqt"], p["b_wuk"], p["b_wuvt"],
               p["b_gq"], p["b_gk"])
    in_specs = ([pl.BlockSpec((ts, D_MODEL), tok)] + [_layer_spec(w, layer) for w in weights]
                + [pl.BlockSpec((ts, HEAD_PAD), tok)] * 2
                + [pl.BlockSpec((ROPE_HALF, ts), lambda i: (0, i))] * 2)
    out_shape = (
        jax.ShapeDtypeStruct((n_tok, A_WIDTH), BF16),
        jax.ShapeDtypeStruct((n_tok // tm, QK_PAD, tm), BF16),
        jax.ShapeDtypeStruct((n_tok, QK_PAD), BF16),
        jax.ShapeDtypeStruct((n_tok // tm, VT_ROWS, tm), BF16),
        jax.ShapeDtypeStruct((n_tok, B_WIDTH), F32),
        jax.ShapeDtypeStruct((seq, bsz * C_WIDTH), F32),
        jax.ShapeDtypeStruct((seq, bsz * C_WIDTH), F32),
        jax.ShapeDtypeStruct((n_tok, M_WIDTH), F32),
        jax.ShapeDtypeStruct((n_tok, M_WIDTH), F32),
    )
    out_specs = (
        pl.BlockSpec((ts, A_WIDTH), tok),
        pl.BlockSpec((SUB_TILES, QK_PAD, tm), lambda i: (i, 0, 0)),
        pl.BlockSpec((ts, QK_PAD), tok),
        pl.BlockSpec((SUB_TILES, VT_ROWS, tm), lambda i: (i, 0, 0)),
        pl.BlockSpec((ts, B_WIDTH), tok),
        pl.BlockSpec((ts, C_WIDTH), seq_major),
        pl.BlockSpec((ts, C_WIDTH), seq_major),
        pl.BlockSpec((ts, M_WIDTH), tok),
        pl.BlockSpec((ts, M_WIDTH), tok),
    )
    return pl.pallas_call(
        _inproj_step,
        out_shape=out_shape,
        grid=(n_tok // ts,),
        in_specs=in_specs,
        out_specs=out_specs,
        compiler_params=pltpu.CompilerParams(
            dimension_semantics=("parallel",), vmem_limit_bytes=V7X_VMEM_LIMIT_BYTES),
        name="inproj",
    )(x2, *weights, *tabs)


def _attn_body(qt_ref, k_ref, vt_ref, g_ref, o_ref, m_ref, a_ref, acc_ref, ot_ref, s_ref):
    tq = qt_ref.shape[1]
    qi = pl.program_id(1)
    key_le_query = (lax.broadcasted_iota(jnp.int32, (tq, tq), 0)
                    <= lax.broadcasted_iota(jnp.int32, (tq, tq), 1))
    m_ref[1] = jnp.full(m_ref.shape[1:], -jnp.inf, F32)
    acc_ref[...] = jnp.zeros(acc_ref.shape, F32)

    def scores(j, masked, slot):
        r0 = pl.multiple_of(j * tq, tq)
        for hd in range(B_HEADS):
            lanes = slice(hd * HEAD_PAD, (hd + 1) * HEAD_PAD)
            st = _dot(k_ref[pl.ds(r0, tq), lanes], qt_ref[lanes, :])
            if masked:
                st = jnp.where(key_le_query, st, -jnp.inf)
            s_ref[slot, hd] = st
            m_old = m_ref[1 - slot, hd:hd + 1, :]
            m_new = jnp.maximum(m_old, jnp.max(st, axis=0, keepdims=True))
            m_ref[slot, hd:hd + 1, :] = m_new
            a_ref[slot, hd:hd + 1, :] = jnp.exp2(m_old - m_new)

    def accumulate(j, slot):
        for hd in range(B_HEADS):
            rows = slice(hd * V_AUG, (hd + 1) * V_AUG)
            pt = jnp.exp2(s_ref[slot, hd] - m_ref[slot, hd:hd + 1, :])
            acc_ref[rows, :] = (a_ref[slot, hd:hd + 1, :] * acc_ref[rows, :]
                                + _dot(vt_ref[j, rows, :], pt.astype(BF16)))

    @pl.when(qi == 0)
    def _():
        scores(0, True, 0)
        accumulate(0, 0)

    @pl.when(qi > 0)
    def _():
        scores(0, False, 0)

        def two_blocks(i, carry):
            scores(2 * i + 1, False, 1)
            accumulate(2 * i, 0)
            scores(2 * i + 2, False, 0)
            accumulate(2 * i + 1, 1)
            return carry

        lax.fori_loop(0, (qi - 1) // 2, two_blocks, 0)

        @pl.when(qi % 2 == 0)
        def _():
            scores(qi - 1, False, 1)
            accumulate(qi - 2, 0)
            scores(qi, True, 0)
            accumulate(qi - 1, 1)
            accumulate(qi, 0)

        @pl.when(qi % 2 == 1)
        def _():
            scores(qi, True, 1)
            accumulate(qi - 1, 0)
            accumulate(qi, 1)

    for hd in range(B_HEADS):
        r0 = hd * V_AUG
        denom = acc_ref[r0 + B_VDIM:r0 + B_VDIM + 1, :]
        ot_ref[hd * B_VDIM:(hd + 1) * B_VDIM, :] = acc_ref[r0:r0 + B_VDIM, :] * (1.0 / denom)
    o_ref[...] = (ot_ref[...].T * g_ref[...]).astype(BF16)


def _attention(q, k, vt, gate, bsz, seq):
    tq = ATTN_TILE
    nq = seq // tq
    return pl.pallas_call(
        _attn_body,
        out_shape=jax.ShapeDtypeStruct((bsz * seq, B_WIDTH), BF16),
        grid=(bsz, nq),
        in_specs=[pl.BlockSpec((None, QK_PAD, tq), lambda b, i: (b * nq + i, 0, 0)),
                  pl.BlockSpec((seq, QK_PAD), lambda b, i: (b, 0)),
                  pl.BlockSpec((nq, VT_ROWS, tq), lambda b, i: (b, 0, 0)),
                  pl.BlockSpec((tq, B_WIDTH), lambda b, i: (b * nq + i, 0))],
        out_specs=pl.BlockSpec((tq, B_WIDTH), lambda b, i: (b * nq + i, 0)),
        scratch_shapes=[pltpu.VMEM((2, B_HEADS, tq), F32), pltpu.VMEM((2, B_HEADS, tq), F32),
                        pltpu.VMEM((VT_ROWS, tq), F32), pltpu.VMEM((B_WIDTH, tq), F32),
                        pltpu.VMEM((2, B_HEADS, tq, tq), F32)],
        compiler_params=pltpu.CompilerParams(
            dimension_semantics=("parallel", "parallel"),
            vmem_limit_bytes=V7X_VMEM_LIMIT_BYTES),
        name="latent_attention",
    )(q, k, vt, gate)


def _s5_discretise_body(are_ref, aim_ref, ldt_ref, bre_ref, bim_ref,
                        abar_re_ref, abar_im_ref, bbar_re_ref, bbar_im_ref):
    a_re, a_im = are_ref[...], aim_ref[...]
    dt = jnp.exp(ldt_ref[...])
    mag = jnp.exp(a_re * dt)
    abar_re = mag * jnp.cos(a_im * dt)
    abar_im = mag * jnp.sin(a_im * dt)
    num_re = abar_re - 1.0
    inv_den = 1.0 / (a_re * a_re + a_im * a_im)
    q_re = (num_re * a_re + abar_im * a_im) * inv_den
    q_im = (abar_im * a_re - num_re * a_im) * inv_den
    b_re, b_im = bre_ref[...], bim_ref[...]
    abar_re_ref[...] = abar_re
    abar_im_ref[...] = abar_im
    bbar_re_ref[...] = q_re * b_re - q_im * b_im
    bbar_im_ref[...] = q_re * b_im + q_im * b_re


def _s5_discretise(a_re, a_im, log_dt, b_re, b_im):
    out = jax.ShapeDtypeStruct(a_re.shape, F32)
    return pl.pallas_call(_s5_discretise_body, out_shape=(out,) * 4, name="s5_discretise")(
        a_re, a_im, log_dt, b_re, b_im)


def _s5_body(u_ref, cg_ref, bmat_ref, cmat_ref, are_ref, aim_ref, d_ref, wglu_ref, bglu_ref,
             y_ref, state_ref, buf_ref):
    lt, bsz, width = u_ref.shape
    rows = lt * bsz
    lb_w = S5_LANE_BLOCK

    @pl.when(pl.program_id(0) == 0)
    def _():
        state_ref[...] = jnp.zeros_like(state_ref)

    u2 = u_ref[...].reshape(rows, width)
    ub = u2.astype(BF16)
    half = rows // 2
    y_halves = [None, None]
    for lb in range(C_STATES // lb_w):
        cols = slice(2 * lb * lb_w, 2 * (lb + 1) * lb_w)
        re_l = slice(2 * lb * lb_w, (2 * lb + 1) * lb_w)
        im_l = slice((2 * lb + 1) * lb_w, 2 * (lb + 1) * lb_w)
        buf_ref[:, cols] = _dot(ub, bmat_ref[:, cols])
        a_re = jnp.broadcast_to(are_ref[:, lb * lb_w:(lb + 1) * lb_w], (bsz, lb_w))
        a_im = jnp.broadcast_to(aim_ref[:, lb * lb_w:(lb + 1) * lb_w], (bsz, lb_w))
        s_re, s_im = state_ref[:, re_l], state_ref[:, im_l]
        for t in range(lt):
            r = slice(t * bsz, (t + 1) * bsz)
            n_re = a_re * s_re - a_im * s_im + buf_ref[r, re_l]
            n_im = a_re * s_im + a_im * s_re + buf_ref[r, im_l]
            buf_ref[r, re_l] = n_re
            buf_ref[r, im_l] = n_im
            s_re, s_im = n_re, n_im
        state_ref[:, re_l] = s_re
        state_ref[:, im_l] = s_im
        for hf in range(2):
            r = slice(hf * half, (hf + 1) * half)
            part = _dot(buf_ref[r, cols].astype(BF16), cmat_ref[cols, :])
            y_halves[hf] = part if y_halves[hf] is None else y_halves[hf] + part

    y = jnp.concatenate(y_halves, axis=0) + d_ref[...] * u2
    y = jax.nn.gelu(y)
    y = y * _sigmoid(_dot(y.astype(BF16), wglu_ref[...]) + bglu_ref[...])
    y_ref[...] = (y * cg_ref[...].reshape(rows, width)).reshape(lt, bsz, width)


def _s5(c_in, c_gate, p, layer, bsz, seq):
    lt = S5_CHUNK
    u3 = c_in.reshape(seq, bsz, C_WIDTH)
    g3 = c_gate.reshape(seq, bsz, C_WIDTH)

    def chunk(i):
        return (i, 0, 0)

    weights = (p["c_bmat"], p["c_cmat"], p["c_are"], p["c_aim"], p["c_d"], p["c_w_glu"],
               p["c_b_glu"])
    y = pl.pallas_call(
        _s5_body,
        out_shape=jax.ShapeDtypeStruct((seq, bsz, C_WIDTH), F32),
        grid=(seq // lt,),
        in_specs=[pl.BlockSpec((lt, bsz, C_WIDTH), chunk)] * 2
        + [_layer_spec(w, layer) for w in weights],
        out_specs=pl.BlockSpec((lt, bsz, C_WIDTH), chunk),
        scratch_shapes=[pltpu.VMEM((bsz, 2 * C_STATES), F32),
                        pltpu.VMEM((lt * bsz, 2 * C_STATES), F32)],
        compiler_params=pltpu.CompilerParams(
            dimension_semantics=("arbitrary",), vmem_limit_bytes=V7X_VMEM_LIMIT_BYTES),
        name="s5_scan",
    )(u3, g3, *weights)
    return y.reshape(seq, bsz * C_WIDTH)


def _head_rms(x, gain, n_heads, head_dim):
    head_id = lax.broadcasted_iota(jnp.int32, x.shape, 1) // head_dim
    out = jnp.zeros_like(x)
    for hd in range(n_heads):
        xh = jnp.where(head_id == hd, x, 0.0)
        ssq = jnp.sum(xh * xh, axis=-1, keepdims=True)
        out = out + xh * lax.rsqrt(ssq * (1.0 / head_dim) + EPS)
    return out * gain


def _memkv_body(mem_ref, ng_ref, wkv_ref, gk_ref, k_ref, v_ref):
    mem_len = mem_ref.shape[0]
    mh = _rms(mem_ref[...], ng_ref[...]).astype(BF16)
    kv = _dot(mh, wkv_ref[...])
    k = _head_rms(kv[:, :M_WIDTH], gk_ref[...], M_HEADS, M_HEAD_DIM)
    v = kv[:, M_WIDTH:]
    head_id = lax.broadcasted_iota(jnp.int32, k.shape, 1) // M_HEAD_DIM
    for hd in range(M_HEADS):
        rows = slice(hd * mem_len, (hd + 1) * mem_len)
        k_ref[rows, :] = jnp.where(head_id == hd, k, 0.0).astype(BF16)
        v_ref[rows, :] = jnp.where(head_id == hd, v, 0.0).astype(BF16)


def _memkv(mem2, p, layer, bsz, mem_len):
    weights = (p["m_norm_g"], p["m_w_kv"], p["m_gk"])
    out = jax.ShapeDtypeStruct((bsz * M_HEADS * mem_len, M_WIDTH), BF16)
    return pl.pallas_call(
        _memkv_body,
        out_shape=(out, out),
        grid=(bsz,),
        in_specs=[pl.BlockSpec((mem_len, D_MODEL), lambda b: (b, 0))]
        + [_layer_spec(w, layer) for w in weights],
        out_specs=(pl.BlockSpec((M_HEADS * mem_len, M_WIDTH), lambda b: (b, 0)),) * 2,
        compiler_params=pltpu.CompilerParams(dimension_semantics=("parallel",)),
        name="memory_kv",
    )(mem2, *weights)


def _merge_body(x_ref, ng_ref, wm_ref, bm_ref, ya_ref, yb_ref, yc_ref, mq_ref, mg_ref,
                mk_ref, mv_ref, gq_ref, wa_ref, wb_ref, wc_ref, wmm_ref, wo_ref, o_ref):
    x = x_ref[...]
    h = _rms(x, ng_ref[...]).astype(BF16)

    def gated(br, y, w_ref):
        cols = slice(br * D_MODEL, (br + 1) * D_MODEL)
        gate = _sigmoid(_dot(h, wm_ref[:, cols]) + bm_ref[:, cols])
        return gate * _dot(y, w_ref[...])

    mq = mq_ref[...]
    mem_len = mk_ref.shape[0] // M_HEADS
    same_head = (lax.broadcasted_iota(jnp.int32, (M_WIDTH, M_WIDTH), 0) // M_HEAD_DIM
                 == lax.broadcasted_iota(jnp.int32, (M_WIDTH, M_WIDTH), 1) // M_HEAD_DIM)
    head_ones = jnp.where(same_head, 1.0, 0.0).astype(BF16)
    q_sq = mq * mq
    q_sq_hi = q_sq.astype(BF16)
    q_sq_lo = (q_sq - q_sq_hi.astype(F32)).astype(BF16)
    ssq = _dot(q_sq_hi, head_ones) + _dot(q_sq_lo, head_ones)
    merged = gated(0, ya_ref[...], wa_ref)
    qn = (mq * lax.rsqrt(ssq * (1.0 / M_HEAD_DIM) + EPS) * gq_ref[...]).astype(BF16)
    s_all = _dot_nt(qn, mk_ref[...])
    merged = merged + gated(1, yb_ref[...], wb_ref)
    probs = []
    for hd in range(M_HEADS):
        s = s_all[:, hd * mem_len:(hd + 1) * mem_len]
        e = jnp.exp(s - jnp.max(s, axis=-1, keepdims=True))
        probs.append((e * (1.0 / jnp.sum(e, axis=-1, keepdims=True))).astype(BF16))
    om = _dot(jnp.concatenate(probs, axis=1), mv_ref[...])
    merged = merged + gated(2, yc_ref[...].astype(BF16), wc_ref)
    ym = (om * mg_ref[...]).astype(BF16)
    merged = merged + gated(3, ym, wmm_ref)
    o_ref[...] = x + _dot(merged.astype(BF16), wo_ref[...])


def _merge_step(x_ref, ng_ref, wm_ref, bm_ref, ya_ref, yb_ref, yc_ref, mq_ref, mg_ref,
                mk_ref, mv_ref, gq_ref, wa_ref, wb_ref, wc_ref, wmm_ref, wo_ref, o_ref):
    tm = TOKEN_TILE
    for sub in range(SUB_TILES):
        rows = pl.ds(sub * tm, tm)
        _merge_body(x_ref.at[rows], ng_ref, wm_ref, bm_ref, ya_ref.at[rows], yb_ref.at[rows],
                    yc_ref.at[rows], mq_ref.at[rows], mg_ref.at[rows], mk_ref, mv_ref, gq_ref,
                    wa_ref, wb_ref, wc_ref, wmm_ref, wo_ref, o_ref.at[rows])


def _merge(x2, ya, yb, yc, mq, mg, mk, mv, p, layer, bsz, seq, mem_len):
    n_tok = x2.shape[0]
    ts = SUB_TILES * TOKEN_TILE
    nst = seq // ts

    def tok(i):
        return (i, 0)

    def seq_major(i):
        return (i % nst, i // nst)

    def per_batch(i):
        return (i // nst, 0)

    def resident(a):
        return _layer_spec(a, layer, pipeline_mode=pl.Buffered(1))

    in_specs = [
        pl.BlockSpec((ts, D_MODEL), tok),
        resident(p["norm_g"]), resident(p["w_merge"]), resident(p["b_merge"]),
        pl.BlockSpec((ts, A_WIDTH), tok),
        pl.BlockSpec((ts, B_WIDTH), tok),
        pl.BlockSpec((ts, C_WIDTH), seq_major),
        pl.BlockSpec((ts, M_WIDTH), tok),
        pl.BlockSpec((ts, M_WIDTH), tok),
        pl.BlockSpec((M_HEADS * mem_len, M_WIDTH), per_batch),
        pl.BlockSpec((M_HEADS * mem_len, M_WIDTH), per_batch),
        resident(p["m_gq"]), resident(p["w_br_a"]), resident(p["w_br_b"]),
        resident(p["w_br_c"]), resident(p["w_br_m"]), resident(p["w_out"]),
    ]
    return pl.pallas_call(
        _merge_step,
        out_shape=jax.ShapeDtypeStruct((n_tok, D_MODEL), F32),
        grid=(n_tok // ts,),
        in_specs=in_specs,
        out_specs=pl.BlockSpec((ts, D_MODEL), tok),
        compiler_params=pltpu.CompilerParams(
            dimension_semantics=("parallel",), vmem_limit_bytes=V7X_VMEM_LIMIT_BYTES),
        name="merge",
    )(x2, p["norm_g"], p["w_merge"], p["b_merge"], ya, yb, yc, mq, mg, mk, mv,
      p["m_gq"], p["w_br_a"], p["w_br_b"], p["w_br_c"], p["w_br_m"], p["w_out"])


W_IN_KROPE = 1792
W_IN_BGATE = W_IN_KROPE + B_ROPE
W_IN_MERGE = 3360
CAST_ROWS = 256


def _cast_in_body(w_ref, ws_ref, wm_ref):
    def put(dst, src, width):
        ws_ref[:, dst:dst + width] = w_ref[:, src:src + width].astype(BF16)

    put(COL_CKV, 1536, B_KV_RANK)
    put(COL_CQ, 768, B_Q_RANK)
    put(COL_AV, 256, A_WIDTH)
    put(COL_AU, 0, A_WIDTH)
    put(COL_AG, 512, A_WIDTH)
    tail = N_SMALL - COL_BG
    span = w_ref[:, W_IN_KROPE:W_IN_KROPE + tail + HEAD_PAD]
    ws_ref[:, COL_BG:] = span[:, B_ROPE:B_ROPE + tail].astype(BF16)
    src = lax.broadcasted_iota(jnp.int32, (B_ROPE, HEAD_PAD), 0)
    dst = lax.broadcasted_iota(jnp.int32, (B_ROPE, HEAD_PAD), 1)
    lane_of = jnp.where(src < ROPE_HALF, ROPE_LO + src, ROPE_HI + src - ROPE_HALF)
    place = jnp.where(dst == lane_of, 1.0, 0.0).astype(BF16)
    ws_ref[:, COL_KPE:COL_KPE + HEAD_PAD] = _dot(span[:, :B_ROPE].astype(BF16), place).astype(BF16)
    merge_lo = W_IN_MERGE - W_IN_MERGE % HEAD_PAD
    span = w_ref[:, merge_lo:]
    off = W_IN_MERGE - merge_lo
    wm_ref[...] = span[:, off:off + N_BRANCH * D_MODEL].astype(BF16)


def _cast_in_weights(w_in):
    depth, d_model, in_width = w_in.shape
    return pl.pallas_call(
        _cast_in_body,
        out_shape=(jax.ShapeDtypeStruct((depth, d_model, N_SMALL), BF16),
                   jax.ShapeDtypeStruct((depth, d_model, N_BRANCH * D_MODEL), BF16)),
        grid=(depth, d_model // CAST_ROWS),
        in_specs=[pl.BlockSpec((None, CAST_ROWS, in_width), lambda l, r: (l, r, 0))],
        out_specs=(pl.BlockSpec((None, CAST_ROWS, N_SMALL), lambda l, r: (l, r, 0)),
                   pl.BlockSpec((None, CAST_ROWS, N_BRANCH * D_MODEL), lambda l, r: (l, r, 0))),
        compiler_params=pltpu.CompilerParams(
            dimension_semantics=("parallel", "parallel"),
            vmem_limit_bytes=V7X_VMEM_LIMIT_BYTES),
        name="cast_in_weights",
    )(w_in)


def _row(v):
    return v.reshape(1, -1).astype(F32)


def _prep_layer(norm_g, b_merge, a_norm_g, a_w_s, a_b_s, b_q_norm_g, b_kv_norm_g,
                b_w_uq, b_w_ukv, b_qk_g_q, b_qk_g_k, c_a_re, c_a_im, c_log_dt, c_b_re,
                c_b_im, c_c_re, c_c_im, c_d, c_w_glu, c_b_glu, m_norm_g, m_w_kv, m_qk_g_q,
                m_qk_g_k, w_br_a, w_br_b, w_br_c, w_br_m, w_out):
    p = {}
    p["norm_g"] = _row(norm_g)
    p["b_merge"] = _row(b_merge)

    p["a_norm_g"] = _row(a_norm_g)
    causal = jnp.tril(jnp.ones((A_CHUNK, A_CHUNK), dtype=bool))
    p["a_wst"] = jnp.where(causal, a_w_s, 0.0).reshape(A_GROUPS * A_CHUNK, A_CHUNK).astype(BF16)
    p["a_bias"] = jnp.repeat(a_b_s.T, A_GROUP_DIM, axis=1).astype(F32)

    p["b_q_norm_g"] = _row(b_q_norm_g)
    p["b_kv_norm_g"] = _row(b_kv_norm_g)
    wuq = _head_layout(b_w_uq.reshape(B_Q_RANK, B_HEADS, B_QK_DIM))
    p["b_wuqt"] = wuq.reshape(B_Q_RANK, QK_PAD).T.astype(BF16)
    wukv = b_w_ukv.reshape(B_KV_RANK, B_HEADS, B_NOPE + B_VDIM)
    wuk = _head_layout(jnp.pad(wukv[:, :, :B_NOPE], ((0, 0), (0, 0), (0, B_ROPE))))
    p["b_wuk"] = wuk.reshape(B_KV_RANK, QK_PAD).astype(BF16)
    wuv = jnp.pad(wukv[:, :, B_NOPE:], ((0, 0), (0, 0), (0, V_AUG - B_VDIM)))
    p["b_wuvt"] = wuv.reshape(B_KV_RANK, VT_ROWS).T.astype(BF16)
    p["b_gq"] = jnp.broadcast_to(_head_layout(b_qk_g_q.astype(F32))[:, None],
                                 (HEAD_PAD, TOKEN_TILE))
    p["b_gk"] = _row(_head_layout(b_qk_g_k))

    def rep(a):
        return jnp.repeat(a.astype(F32), C_GROUP, axis=0)

    def gcp(b):
        return b.astype(F32).transpose(0, 2, 1).reshape(C_WIDTH, C_STATE)

    log_dt = jnp.broadcast_to(c_log_dt.astype(F32)[:, None], (C_GROUPS, C_STATE))
    abar_re, abar_im, bbar_re, bbar_im = _s5_discretise(
        rep(c_a_re), rep(c_a_im), rep(log_dt), gcp(c_b_re), gcp(c_b_im))
    eye = jnp.eye(C_GROUPS, dtype=F32)

    def expand_in(m):
        m = m.reshape(C_GROUPS, C_GROUP, C_STATE)
        return (eye[:, None, :, None] * m[:, :, None, :]).reshape(C_WIDTH, C_STATES)

    def expand_out(m):
        return (eye[:, None, :, None] * m.transpose(0, 2, 1)[:, :, None, :]).reshape(
            C_STATES, C_WIDTH)

    n_lb = C_STATES // S5_LANE_BLOCK
    bmat = jnp.stack([expand_in(bbar_re).reshape(C_WIDTH, n_lb, S5_LANE_BLOCK),
                      expand_in(bbar_im).reshape(C_WIDTH, n_lb, S5_LANE_BLOCK)], axis=2)
    p["c_bmat"] = bmat.reshape(C_WIDTH, 2 * C_STATES).astype(BF16)
    cmat = jnp.stack([expand_out(c_c_re.astype(F32)).reshape(n_lb, S5_LANE_BLOCK, C_WIDTH),
                      -expand_out(c_c_im.astype(F32)).reshape(n_lb, S5_LANE_BLOCK, C_WIDTH)],
                     axis=1)
    p["c_cmat"] = cmat.reshape(2 * C_STATES, C_WIDTH).astype(BF16)
    p["c_are"] = _row(abar_re.reshape(C_GROUPS, C_GROUP, C_STATE)[:, 0, :])
    p["c_aim"] = _row(abar_im.reshape(C_GROUPS, C_GROUP, C_STATE)[:, 0, :])
    p["c_d"] = _row(c_d)
    p["c_w_glu"] = c_w_glu.astype(BF16)
    p["c_b_glu"] = _row(c_b_glu)

    p["m_norm_g"] = _row(m_norm_g)
    p["m_w_kv"] = m_w_kv.astype(BF16)
    p["m_gq"] = _row(jnp.tile(m_qk_g_q, M_HEADS)) * (M_HEAD_DIM ** -0.5)
    p["m_gk"] = _row(jnp.tile(m_qk_g_k, M_HEADS))
    p["w_br_a"] = w_br_a.astype(BF16)
    p["w_br_b"] = w_br_b.astype(BF16)
    p["w_br_c"] = w_br_c.astype(BF16)
    p["w_br_m"] = w_br_m.astype(BF16)
    p["w_out"] = w_out.astype(BF16)
    return p


def kernel(x, mem, positions, norm_g, w_in, b_merge, a_norm_g, a_w_s, a_b_s, b_q_norm_g, b_kv_norm_g, b_w_uq, b_w_ukv, b_qk_g_q, b_qk_g_k, c_a_re, c_a_im, c_log_dt, c_b_re, c_b_im, c_c_re, c_c_im, c_d, c_w_glu, c_b_glu, m_norm_g, m_w_kv, m_qk_g_q, m_qk_g_k, w_br_a, w_br_b, w_br_c, w_br_m, w_out):
    bsz, seq, d_model = x.shape
    mem_len = mem.shape[1]
    depth = norm_g.shape[0]
    assert d_model == D_MODEL and seq % (SUB_TILES * TOKEN_TILE) == 0
    assert seq % S5_CHUNK == 0 and bsz == 8 and TOKEN_TILE == ATTN_TILE

    stacked = (norm_g, b_merge, a_norm_g, a_w_s, a_b_s, b_q_norm_g, b_kv_norm_g,
               b_w_uq, b_w_ukv, b_qk_g_q, b_qk_g_k, c_a_re, c_a_im, c_log_dt, c_b_re,
               c_b_im, c_c_re, c_c_im, c_d, c_w_glu, c_b_glu, m_norm_g, m_w_kv, m_qk_g_q,
               m_qk_g_k, w_br_a, w_br_b, w_br_c, w_br_m, w_out)

    tabs = _rope_tables(positions)
    x2 = x.reshape(bsz * seq, d_model)
    mem2 = mem.reshape(bsz * mem_len, d_model)
    p = jax.vmap(_prep_layer)(*stacked)
    p["w_small"], p["w_merge"] = _cast_in_weights(w_in)
    for layer in range(depth):
        ya, q, k, vt, bg, ci, cg, mq, mg = _inproj(x2, p, layer, tabs, bsz, seq)
        yb = _attention(q, k, vt, bg, bsz, seq)
        yc = _s5(ci, cg, p, layer, bsz, seq)
        mk, mv = _memkv(mem2, p, layer, bsz, mem_len)
        x2 = _merge(x2, ya, yb, yc, mq, mg, mk, mv, p, layer, bsz, seq, mem_len)
    return x2.reshape(bsz, seq, d_model)
```

```python
import functools
import math

import jax
import jax.numpy as jnp
from jax import lax
from jax.experimental import pallas as pl
from jax.experimental.pallas import tpu as pltpu

F32 = jnp.float32
BF16 = jnp.bfloat16

D_MODEL = 1024
EPS = 1e-6
N_BRANCH = 4

A_WIDTH = 256
A_GROUPS = 4
A_GROUP_DIM = A_WIDTH // A_GROUPS
A_CHUNK = 128

B_HEADS = 8
B_NOPE = 64
B_ROPE = 32
B_QK_DIM = B_NOPE + B_ROPE
B_VDIM = 64
B_Q_RANK = 768
B_KV_RANK = 256
B_WIDTH = B_HEADS * B_VDIM
ROPE_THETA = 10000.0
HEAD_PAD = 128
QK_PAD = B_HEADS * HEAD_PAD
V_AUG = B_VDIM + 16
VT_ROWS = B_HEADS * V_AUG

C_WIDTH = 256
C_GROUP = 16
C_GROUPS = C_WIDTH // C_GROUP
C_STATE = 64
C_STATES = C_GROUPS * C_STATE

M_HEADS = 4
M_HEAD_DIM = 64
M_WIDTH = M_HEADS * M_HEAD_DIM

COL_CKV = 0
COL_KPE = 256
COL_CQ = 384
COL_AV = 1152
COL_AU = 1408
COL_AG = 1664
COL_BG = 1920
COL_CI = 2432
COL_CG = 2688
COL_MQ = 2944
COL_MG = 3200
N_SMALL = 3456

V7X_VMEM_LIMIT_BYTES = 56 * 1024 * 1024

TOKEN_TILE = 256
SUB_TILES = 2
ATTN_TILE = 256
S5_CHUNK = 64
S5_LANE_BLOCK = 256


def _sigmoid(x):
    return 1.0 / (1.0 + jnp.exp(-x))


def _silu(x):
    return x * _sigmoid(x)


def _rms(x, g):
    return x * lax.rsqrt(jnp.mean(x * x, axis=-1, keepdims=True) + EPS) * g


def _dot(a, b):
    return jnp.dot(a, b, preferred_element_type=F32)


def _dot_nt(a, b):
    return lax.dot_general(a, b, (((1,), (1,)), ((), ())), preferred_element_type=F32)


def _layer_spec(a, layer, **kwargs):
    zeros = (0,) * (a.ndim - 1)
    return pl.BlockSpec((None,) + a.shape[1:], lambda *_: (layer,) + zeros, **kwargs)


ROPE_HALF = B_ROPE // 2
ROPE_LO = 0
ROPE_HI = HEAD_PAD // 2


def _head_layout(w):
    nope, rope = w[..., :B_NOPE], w[..., B_NOPE:]
    split = ROPE_HI - ROPE_HALF
    pad = jnp.zeros(w.shape[:-1] + (HEAD_PAD - B_QK_DIM,), w.dtype)
    return jnp.concatenate([rope[..., :ROPE_HALF], nope[..., :split], rope[..., ROPE_HALF:],
                            nope[..., split:], pad], axis=-1)


def _rope_table_body(pos_row_ref, freq_col_ref, c_ref, s_ref, ct_ref, st_ref):
    ang_t = freq_col_ref[...] * pos_row_ref[...]
    cos_t = jnp.cos(ang_t)
    sin_t = jnp.sin(ang_t)
    ct_ref[...] = cos_t
    st_ref[...] = sin_t
    gap = ROPE_HI - ROPE_HALF
    ones = jnp.ones((gap, ang_t.shape[1]), F32)
    zeros = jnp.zeros((gap, ang_t.shape[1]), F32)
    c_ref[...] = jnp.concatenate([cos_t, ones, cos_t, ones], axis=0).T
    s_ref[...] = jnp.concatenate([-sin_t, zeros, sin_t, zeros], axis=0).T


def _rope_tables(positions):
    n_tok = positions.size
    inv_freq = ROPE_THETA ** (-jnp.arange(ROPE_HALF, dtype=F32) / ROPE_HALF)
    tm = 1024
    freq_col = jnp.broadcast_to(inv_freq[:, None], (ROPE_HALF, tm))
    pos = positions.astype(F32)
    tab = jax.ShapeDtypeStruct((n_tok, HEAD_PAD), F32)
    tab_t = jax.ShapeDtypeStruct((ROPE_HALF, n_tok), F32)
    return pl.pallas_call(
        _rope_table_body,
        out_shape=(tab, tab, tab_t, tab_t),
        grid=(n_tok // tm,),
        in_specs=[pl.BlockSpec((1, tm), lambda i: (0, i)),
                  pl.BlockSpec((ROPE_HALF, tm), lambda i: (0, 0))],
        out_specs=(pl.BlockSpec((tm, HEAD_PAD), lambda i: (i, 0)),) * 2
        + (pl.BlockSpec((ROPE_HALF, tm), lambda i: (0, i)),) * 2,
        compiler_params=pltpu.CompilerParams(dimension_semantics=("parallel",)),
        name="rope_tables",
    )(pos.reshape(1, n_tok), freq_col)


def _inproj_body(x_ref, ng_ref, w_ref, ang_ref, wst_ref, bias_ref, qng_ref, kvng_ref,
                 wuqt_ref, wuk_ref, wuvt_ref, gq_ref, gk_ref, c_ref, s_ref, ct_ref, st_ref,
                 ya_ref, qt_ref, k_ref, vt_ref, bg_ref, ci_ref, cg_ref, mq_ref, mg_ref):
    tm = x_ref.shape[0]
    h = _rms(x_ref[...], ng_ref[...]).astype(BF16)

    z_lat = _dot_nt(h, w_ref[:COL_AV, :])

    def lat(lo, width):
        return z_lat[:, lo:lo + width]

    ckv = _rms(lat(COL_CKV, B_KV_RANK), kvng_ref[...]).astype(BF16)
    kf = _dot(ckv, wuk_ref[...])
    z_rest = _dot_nt(h, w_ref[COL_AV:, :])

    def proj(lo, width):
        return z_rest[:, lo - COL_AV:lo - COL_AV + width]

    kpe = lat(COL_KPE, HEAD_PAD)
    c_tab, s_tab = c_ref[...], s_ref[...]
    for hd in range(B_HEADS):
        lanes = slice(hd * HEAD_PAD, (hd + 1) * HEAD_PAD)
        kh = kf[:, lanes] + kpe
        ssq = jnp.sum(kh * kh, axis=-1, keepdims=True)
        kh = kh * lax.rsqrt(ssq * (1.0 / B_QK_DIM) + EPS) * gk_ref[...]
        k_ref[:, lanes] = (kh * c_tab + pltpu.roll(kh, HEAD_PAD // 2, 1) * s_tab).astype(BF16)
    vt = _dot_nt(wuvt_ref[...], ckv)
    ones_row = lax.broadcasted_iota(jnp.int32, vt.shape, 0) % V_AUG == B_VDIM
    vt_ref[...] = jnp.where(ones_row, 1.0, vt).astype(BF16)

    u = jax.nn.gelu(proj(COL_AU, A_WIDTH))
    v = _rms(jax.nn.gelu(proj(COL_AV, A_WIDTH)), ang_ref[...]).astype(BF16)
    ag = _silu(proj(COL_AG, A_WIDTH))
    lane_group = lax.broadcasted_iota(jnp.int32, (A_CHUNK, A_WIDTH), 1) // A_GROUP_DIM
    for c in range(tm // A_CHUNK):
        rows = slice(c * A_CHUNK, (c + 1) * A_CHUNK)
        s_all = _dot(wst_ref[...], v[rows, :])
        s = s_all[(A_GROUPS - 1) * A_CHUNK:, :]
        for g in range(A_GROUPS - 2, -1, -1):
            s = jnp.where(lane_group == g, s_all[g * A_CHUNK:(g + 1) * A_CHUNK, :], s)
        s = s + bias_ref[...]
        ya_ref[rows, :] = (u[rows, :] * s * ag[rows, :]).astype(BF16)

    q_scale = B_QK_DIM ** -0.5 * math.log2(math.e)
    cq = _rms(lat(COL_CQ, B_Q_RANK), qng_ref[...]).astype(BF16)
    half_rows = QK_PAD // 2
    qt_halves = [_dot_nt(wuqt_ref[i * half_rows:(i + 1) * half_rows, :], cq) for i in range(2)]
    cos_t, sin_t = ct_ref[...], st_ref[...]
    for hd in range(B_HEADS):
        r0 = hd * HEAD_PAD
        qr = r0 % half_rows
        qh = qt_halves[r0 // half_rows][qr:qr + HEAD_PAD, :]
        ssq = jnp.sum(qh * qh, axis=0, keepdims=True)
        qh = qh * (lax.rsqrt(ssq * (1.0 / B_QK_DIM) + EPS) * q_scale) * gq_ref[...]
        x1 = qh[ROPE_LO:ROPE_LO + ROPE_HALF, :]
        x2 = qh[ROPE_HI:ROPE_HI + ROPE_HALF, :]
        qt_ref[r0:r0 + HEAD_PAD, :] = qh.astype(BF16)
        qt_ref[r0 + ROPE_LO:r0 + ROPE_LO + ROPE_HALF, :] = (x1 * cos_t - x2 * sin_t).astype(BF16)
        qt_ref[r0 + ROPE_HI:r0 + ROPE_HI + ROPE_HALF, :] = (x2 * cos_t + x1 * sin_t).astype(BF16)

    bg_ref[...] = _silu(proj(COL_BG, B_WIDTH))
    ci_ref[...] = proj(COL_CI, C_WIDTH)
    cg_ref[...] = _silu(proj(COL_CG, C_WIDTH))
    mq_ref[...] = proj(COL_MQ, M_WIDTH)
    mg_ref[...] = _silu(proj(COL_MG, M_WIDTH))


def _inproj_step(x_ref, *refs):
    weights = refs[:12]
    c_ref, s_ref, ct_ref, st_ref = refs[12:16]
    ya_ref, qt_ref, k_ref, vt_ref, bg_ref, ci_ref, cg_ref, mq_ref, mg_ref = refs[16:]
    tm = TOKEN_TILE
    for sub in range(SUB_TILES):
        rows = pl.ds(sub * tm, tm)
        _inproj_body(x_ref.at[rows], *weights, c_ref.at[rows], s_ref.at[rows],
                     ct_ref.at[:, rows], st_ref.at[:, rows],
                     ya_ref.at[rows], qt_ref.at[sub], k_ref.at[rows], vt_ref.at[sub],
                     bg_ref.at[rows], ci_ref.at[rows], cg_ref.at[rows], mq_ref.at[rows],
                     mg_ref.at[rows])


def _inproj(x2, p, layer, tabs, bsz, seq):
    n_tok = x2.shape[0]
    tm = TOKEN_TILE
    ts = SUB_TILES * tm
    nst = seq // ts

    def tok(i):
        return (i, 0)

    def seq_major(i):
        return (i % nst, i // nst)

    weights = (p["norm_g"], p["w_small"], p["a_norm_g"], p["a_wst"], p["a_bias"],
               p["b_q_norm_g"], p["b_kv_norm_g"], p["b_wuqt"], p["b_wuk"], p["b_wuvt"],
               p["b_gq"], p["b_gk"])
    in_specs = ([pl.BlockSpec((ts, D_MODEL), tok)] + [_layer_spec(w, layer) for w in weights]
                + [pl.BlockSpec((ts, HEAD_PAD), tok)] * 2
                + [pl.BlockSpec((ROPE_HALF, ts), lambda i: (0, i))] * 2)
    out_shape = (
        jax.ShapeDtypeStruct((n_tok, A_WIDTH), BF16),
        jax.ShapeDtypeStruct((n_tok // tm, QK_PAD, tm), BF16),
        jax.ShapeDtypeStruct((n_tok, QK_PAD), BF16),
        jax.ShapeDtypeStruct((n_tok // tm, VT_ROWS, tm), BF16),
        jax.ShapeDtypeStruct((n_tok, B_WIDTH), F32),
        jax.ShapeDtypeStruct((seq, bsz * C_WIDTH), F32),
        jax.ShapeDtypeStruct((seq, bsz * C_WIDTH), F32),
        jax.ShapeDtypeStruct((n_tok, M_WIDTH), F32),
        jax.ShapeDtypeStruct((n_tok, M_WIDTH), F32),
    )
    out_specs = (
        pl.BlockSpec((ts, A_WIDTH), tok),
        pl.BlockSpec((SUB_TILES, QK_PAD, tm), lambda i: (i, 0, 0)),
        pl.BlockSpec((ts, QK_PAD), tok),
        pl.BlockSpec((SUB_TILES, VT_ROWS, tm), lambda i: (i, 0, 0)),
        pl.BlockSpec((ts, B_WIDTH), tok),
        pl.BlockSpec((ts, C_WIDTH), seq_major),
        pl.BlockSpec((ts, C_WIDTH), seq_major),
        pl.BlockSpec((ts, M_WIDTH), tok),
        pl.BlockSpec((ts, M_WIDTH), tok),
    )
    return pl.pallas_call(
        _inproj_step,
        out_shape=out_shape,
        grid=(n_tok // ts,),
        in_specs=in_specs,
        out_specs=out_specs,
        compiler_params=pltpu.CompilerParams(
            dimension_semantics=("parallel",), vmem_limit_bytes=V7X_VMEM_LIMIT_BYTES),
        name="inproj",
    )(x2, *weights, *tabs)


def _attn_body(qt_ref, k_ref, vt_ref, g_ref, o_ref, m_ref, a_ref, acc_ref, ot_ref, s_ref):
    tq = qt_ref.shape[1]
    qi = pl.program_id(1)
    key_le_query = (lax.broadcasted_iota(jnp.int32, (tq, tq), 0)
                    <= lax.broadcasted_iota(jnp.int32, (tq, tq), 1))
    m_ref[1] = jnp.full(m_ref.shape[1:], -jnp.inf, F32)
    acc_ref[...] = jnp.zeros(acc_ref.shape, F32)

    def scores(j, masked, slot):
        r0 = pl.multiple_of(j * tq, tq)
        for hd in range(B_HEADS):
            lanes = slice(hd * HEAD_PAD, (hd + 1) * HEAD_PAD)
            st = _dot(k_ref[pl.ds(r0, tq), lanes], qt_ref[lanes, :])
            if masked:
                st = jnp.where(key_le_query, st, -jnp.inf)
            s_ref[slot, hd] = st
            m_old = m_ref[1 - slot, hd:hd + 1, :]
            m_new = jnp.maximum(m_old, jnp.max(st, axis=0, keepdims=True))
            m_ref[slot, hd:hd + 1, :] = m_new
            a_ref[slot, hd:hd + 1, :] = jnp.exp2(m_old - m_new)

    def accumulate(j, slot):
        for hd in range(B_HEADS):
            rows = slice(hd * V_AUG, (hd + 1) * V_AUG)
            pt = jnp.exp2(s_ref[slot, hd] - m_ref[slot, hd:hd + 1, :])
            acc_ref[rows, :] = (a_ref[slot, hd:hd + 1, :] * acc_ref[rows, :]
                                + _dot(vt_ref[j, rows, :], pt.astype(BF16)))

    @pl.when(qi == 0)
    def _():
        scores(0, True, 0)
        accumulate(0, 0)

    @pl.when(qi > 0)
    def _():
        scores(0, False, 0)

        def two_blocks(i, carry):
            scores(2 * i + 1, False, 1)
            accumulate(2 * i, 0)
            scores(2 * i + 2, False, 0)
            accumulate(2 * i + 1, 1)
            return carry

        lax.fori_loop(0, (qi - 1) // 2, two_blocks, 0)

        @pl.when(qi % 2 == 0)
        def _():
            scores(qi - 1, False, 1)
            accumulate(qi - 2, 0)
            scores(qi, True, 0)
            accumulate(qi - 1, 1)
            accumulate(qi, 0)

        @pl.when(qi % 2 == 1)
        def _():
            scores(qi, True, 1)
            accumulate(qi - 1, 0)
            accumulate(qi, 1)

    for hd in range(B_HEADS):
        r0 = hd * V_AUG
        denom = acc_ref[r0 + B_VDIM:r0 + B_VDIM + 1, :]
        ot_ref[hd * B_VDIM:(hd + 1) * B_VDIM, :] = acc_ref[r0:r0 + B_VDIM, :] * (1.0 / denom)
    o_ref[...] = (ot_ref[...].T * g_ref[...]).astype(BF16)


def _attention(q, k, vt, gate, bsz, seq):
    tq = ATTN_TILE
    nq = seq // tq
    return pl.pallas_call(
        _attn_body,
        out_shape=jax.ShapeDtypeStruct((bsz * seq, B_WIDTH), BF16),
        grid=(bsz, nq),
        in_specs=[pl.BlockSpec((None, QK_PAD, tq), lambda b, i: (b * nq + i, 0, 0)),
                  pl.BlockSpec((seq, QK_PAD), lambda b, i: (b, 0)),
                  pl.BlockSpec((nq, VT_ROWS, tq), lambda b, i: (b, 0, 0)),
                  pl.BlockSpec((tq, B_WIDTH), lambda b, i: (b * nq + i, 0))],
        out_specs=pl.BlockSpec((tq, B_WIDTH), lambda b, i: (b * nq + i, 0)),
        scratch_shapes=[pltpu.VMEM((2, B_HEADS, tq), F32), pltpu.VMEM((2, B_HEADS, tq), F32),
                        pltpu.VMEM((VT_ROWS, tq), F32), pltpu.VMEM((B_WIDTH, tq), F32),
                        pltpu.VMEM((2, B_HEADS, tq, tq), F32)],
        compiler_params=pltpu.CompilerParams(
            dimension_semantics=("parallel", "parallel"),
            vmem_limit_bytes=V7X_VMEM_LIMIT_BYTES),
        name="latent_attention",
    )(q, k, vt, gate)


def _s5_discretise_body(are_ref, aim_ref, ldt_ref, bre_ref, bim_ref,
                        abar_re_ref, abar_im_ref, bbar_re_ref, bbar_im_ref):
    a_re, a_im = are_ref[...], aim_ref[...]
    dt = jnp.exp(ldt_ref[...])
    mag = jnp.exp(a_re * dt)
    abar_re = mag * jnp.cos(a_im * dt)
    abar_im = mag * jnp.sin(a_im * dt)
    num_re = abar_re - 1.0
    inv_den = 1.0 / (a_re * a_re + a_im * a_im)
    q_re = (num_re * a_re + abar_im * a_im) * inv_den
    q_im = (abar_im * a_re - num_re * a_im) * inv_den
    b_re, b_im = bre_ref[...], bim_ref[...]
    abar_re_ref[...] = abar_re
    abar_im_ref[...] = abar_im
    bbar_re_ref[...] = q_re * b_re - q_im * b_im
    bbar_im_ref[...] = q_re * b_im + q_im * b_re


def _s5_discretise(a_re, a_im, log_dt, b_re, b_im):
    out = jax.ShapeDtypeStruct(a_re.shape, F32)
    return pl.pallas_call(_s5_discretise_body, out_shape=(out,) * 4, name="s5_discretise")(
        a_re, a_im, log_dt, b_re, b_im)


def _s5_body(u_ref, cg_ref, bmat_ref, cmat_ref, are_ref, aim_ref, d_ref, wglu_ref, bglu_ref,
             y_ref, state_ref, buf_ref):
    lt, bsz, width = u_ref.shape
    rows = lt * bsz
    lb_w = S5_LANE_BLOCK

    @pl.when(pl.program_id(0) == 0)
    def _():
        state_ref[...] = jnp.zeros_like(state_ref)

    u2 = u_ref[...].reshape(rows, width)
    ub = u2.astype(BF16)
    half = rows // 2
    y_halves = [None, None]
    for lb in range(C_STATES // lb_w):
        cols = slice(2 * lb * lb_w, 2 * (lb + 1) * lb_w)
        re_l = slice(2 * lb * lb_w, (2 * lb + 1) * lb_w)
        im_l = slice((2 * lb + 1) * lb_w, 2 * (lb + 1) * lb_w)
        buf_ref[:, cols] = _dot(ub, bmat_ref[:, cols])
        a_re = jnp.broadcast_to(are_ref[:, lb * lb_w:(lb + 1) * lb_w], (bsz, lb_w))
        a_im = jnp.broadcast_to(aim_ref[:, lb * lb_w:(lb + 1) * lb_w], (bsz, lb_w))
        s_re, s_im = state_ref[:, re_l], state_ref[:, im_l]
        for t in range(lt):
            r = slice(t * bsz, (t + 1) * bsz)
            n_re = a_re * s_re - a_im * s_im + buf_ref[r, re_l]
            n_im = a_re * s_im + a_im * s_re + buf_ref[r, im_l]
            buf_ref[r, re_l] = n_re
            buf_ref[r, im_l] = n_im
            s_re, s_im = n_re, n_im
        state_ref[:, re_l] = s_re
        state_ref[:, im_l] = s_im
        for hf in range(2):
            r = slice(hf * half, (hf + 1) * half)
            part = _dot(buf_ref[r, cols].astype(BF16), cmat_ref[cols, :])
            y_halves[hf] = part if y_halves[hf] is None else y_halves[hf] + part

    y = jnp.concatenate(y_halves, axis=0) + d_ref[...] * u2
    y = jax.nn.gelu(y)
    y = y * _sigmoid(_dot(y.astype(BF16), wglu_ref[...]) + bglu_ref[...])
    y_ref[...] = (y * cg_ref[...].reshape(rows, width)).reshape(lt, bsz, width)


def _s5(c_in, c_gate, p, layer, bsz, seq):
    lt = S5_CHUNK
    u3 = c_in.reshape(seq, bsz, C_WIDTH)
    g3 = c_gate.reshape(seq, bsz, C_WIDTH)

    def chunk(i):
        return (i, 0, 0)

    weights = (p["c_bmat"], p["c_cmat"], p["c_are"], p["c_aim"], p["c_d"], p["c_w_glu"],
               p["c_b_glu"])
    y = pl.pallas_call(
        _s5_body,
        out_shape=jax.ShapeDtypeStruct((seq, bsz, C_WIDTH), F32),
        grid=(seq // lt,),
        in_specs=[pl.BlockSpec((lt, bsz, C_WIDTH), chunk)] * 2
        + [_layer_spec(w, layer) for w in weights],
        out_specs=pl.BlockSpec((lt, bsz, C_WIDTH), chunk),
        scratch_shapes=[pltpu.VMEM((bsz, 2 * C_STATES), F32),
                        pltpu.VMEM((lt * bsz, 2 * C_STATES), F32)],
        compiler_params=pltpu.CompilerParams(
            dimension_semantics=("arbitrary",), vmem_limit_bytes=V7X_VMEM_LIMIT_BYTES),
        name="s5_scan",
    )(u3, g3, *weights)
    return y.reshape(seq, bsz * C_WIDTH)


def _head_rms(x, gain, n_heads, head_dim):
    head_id = lax.broadcasted_iota(jnp.int32, x.shape, 1) // head_dim
    out = jnp.zeros_like(x)
    for hd in range(n_heads):
        xh = jnp.where(head_id == hd, x, 0.0)
        ssq = jnp.sum(xh * xh, axis=-1, keepdims=True)
        out = out + xh * lax.rsqrt(ssq * (1.0 / head_dim) + EPS)
    return out * gain


def _memkv_body(mem_ref, ng_ref, wkv_ref, gk_ref, k_ref, v_ref):
    mem_len = mem_ref.shape[0]
    mem = mem_ref[...]
    head_id = lax.broadcasted_iota(jnp.int32, (mem_len, M_WIDTH), 1) // M_HEAD_DIM
    for layer in range(ng_ref.shape[0]):
        mh = _rms(mem, ng_ref[layer]).astype(BF16)
        kv = _dot(mh, wkv_ref[layer])
        k = _head_rms(kv[:, :M_WIDTH], gk_ref[layer], M_HEADS, M_HEAD_DIM)
        v = kv[:, M_WIDTH:]
        for hd in range(M_HEADS):
            rows = slice(hd * mem_len, (hd + 1) * mem_len)
            k_ref[layer, rows, :] = jnp.where(head_id == hd, k, 0.0).astype(BF16)
            v_ref[layer, rows, :] = jnp.where(head_id == hd, v, 0.0).astype(BF16)


def _memkv(mem2, p, bsz, mem_len):
    weights = (p["m_norm_g"], p["m_w_kv"], p["m_gk"])
    depth = weights[0].shape[0]
    out = jax.ShapeDtypeStruct((depth, bsz * M_HEADS * mem_len, M_WIDTH), BF16)
    return pl.pallas_call(
        _memkv_body,
        out_shape=(out, out),
        grid=(bsz,),
        in_specs=[pl.BlockSpec((mem_len, D_MODEL), lambda b: (b, 0))]
        + [pl.BlockSpec(w.shape, lambda b: (0, 0, 0)) for w in weights],
        out_specs=(pl.BlockSpec((depth, M_HEADS * mem_len, M_WIDTH), lambda b: (0, b, 0)),) * 2,
        compiler_params=pltpu.CompilerParams(dimension_semantics=("parallel",)),
        name="memory_kv",
    )(mem2, *weights)


def _merge_body(x_ref, ng_ref, wm_ref, bm_ref, ya_ref, yb_ref, yc_ref, mq_ref, mg_ref,
                mk_ref, mv_ref, gq_ref, wa_ref, wb_ref, wc_ref, wmm_ref, wo_ref, o_ref):
    x = x_ref[...]
    h = _rms(x, ng_ref[...]).astype(BF16)

    def gated(br, y, w_ref):
        cols = slice(br * D_MODEL, (br + 1) * D_MODEL)
        gate = _sigmoid(_dot_nt(h, wm_ref[cols, :]) + bm_ref[:, cols])
        return gate * _dot(y, w_ref[...])

    mq = mq_ref[...]
    mem_len = mk_ref.shape[0] // M_HEADS
    same_head = (lax.broadcasted_iota(jnp.int32, (M_WIDTH, M_WIDTH), 0) // M_HEAD_DIM
                 == lax.broadcasted_iota(jnp.int32, (M_WIDTH, M_WIDTH), 1) // M_HEAD_DIM)
    head_ones = jnp.where(same_head, 1.0, 0.0).astype(BF16)
    q_sq = mq * mq
    q_sq_hi = q_sq.astype(BF16)
    q_sq_lo = (q_sq - q_sq_hi.astype(F32)).astype(BF16)
    ssq = _dot(q_sq_hi, head_ones) + _dot(q_sq_lo, head_ones)
    merged = gated(0, ya_ref[...], wa_ref)
    qn = (mq * lax.rsqrt(ssq * (1.0 / M_HEAD_DIM) + EPS) * gq_ref[...]).astype(BF16)
    s_all = _dot_nt(qn, mk_ref[...])
    merged = merged + gated(1, yb_ref[...], wb_ref)
    probs = []
    for hd in range(M_HEADS):
        s = s_all[:, hd * mem_len:(hd + 1) * mem_len]
        e = jnp.exp(s - jnp.max(s, axis=-1, keepdims=True))
        probs.append((e * (1.0 / jnp.sum(e, axis=-1, keepdims=True))).astype(BF16))
    om = _dot(jnp.concatenate(probs, axis=1), mv_ref[...])
    merged = merged + gated(2, yc_ref[...].astype(BF16), wc_ref)
    ym = (om * mg_ref[...]).astype(BF16)
    merged = merged + gated(3, ym, wmm_ref)
    o_ref[...] = x + _dot(merged.astype(BF16), wo_ref[...])


def _merge_step(x_ref, ng_ref, wm_ref, bm_ref, ya_ref, yb_ref, yc_ref, mq_ref, mg_ref,
                mk_ref, mv_ref, gq_ref, wa_ref, wb_ref, wc_ref, wmm_ref, wo_ref, o_ref):
    tm = TOKEN_TILE
    for sub in range(SUB_TILES):
        rows = pl.ds(sub * tm, tm)
        _merge_body(x_ref.at[rows], ng_ref, wm_ref, bm_ref, ya_ref.at[rows], yb_ref.at[rows],
                    yc_ref.at[rows], mq_ref.at[rows], mg_ref.at[rows], mk_ref, mv_ref, gq_ref,
                    wa_ref, wb_ref, wc_ref, wmm_ref, wo_ref, o_ref.at[rows])


def _merge(x2, ya, yb, yc, mq, mg, mk, mv, p, layer, bsz, seq, mem_len):
    n_tok = x2.shape[0]
    ts = SUB_TILES * TOKEN_TILE
    nst = seq // ts

    def tok(i):
        return (i, 0)

    def seq_major(i):
        return (i % nst, i // nst)

    def per_batch(i):
        return (layer, i // nst, 0)

    def resident(a):
        return _layer_spec(a, layer, pipeline_mode=pl.Buffered(1))

    in_specs = [
        pl.BlockSpec((ts, D_MODEL), tok),
        resident(p["norm_g"]), resident(p["w_merge"]), resident(p["b_merge"]),
        pl.BlockSpec((ts, A_WIDTH), tok),
        pl.BlockSpec((ts, B_WIDTH), tok),
        pl.BlockSpec((ts, C_WIDTH), seq_major),
        pl.BlockSpec((ts, M_WIDTH), tok),
        pl.BlockSpec((ts, M_WIDTH), tok),
        pl.BlockSpec((None, M_HEADS * mem_len, M_WIDTH), per_batch),
        pl.BlockSpec((None, M_HEADS * mem_len, M_WIDTH), per_batch),
        resident(p["m_gq"]), resident(p["w_br_a"]), resident(p["w_br_b"]),
        resident(p["w_br_c"]), resident(p["w_br_m"]), resident(p["w_out"]),
    ]
    return pl.pallas_call(
        _merge_step,
        out_shape=jax.ShapeDtypeStruct((n_tok, D_MODEL), F32),
        grid=(n_tok // ts,),
        in_specs=in_specs,
        out_specs=pl.BlockSpec((ts, D_MODEL), tok),
        compiler_params=pltpu.CompilerParams(
            dimension_semantics=("parallel",), vmem_limit_bytes=V7X_VMEM_LIMIT_BYTES),
        name="merge",
    )(x2, p["norm_g"], p["w_merge"], p["b_merge"], ya, yb, yc, mq, mg, mk, mv,
      p["m_gq"], p["w_br_a"], p["w_br_b"], p["w_br_c"], p["w_br_m"], p["w_out"])


W_IN_KROPE = 1792
W_IN_MERGE = 3360
CAST_ROWS = HEAD_PAD
MERGE_CAST_ROWS = 1024
SMALL_SEGMENTS = ((COL_CKV, 1536, B_KV_RANK), (COL_CQ, 768, B_Q_RANK), (COL_AV, 256, A_WIDTH),
                  (COL_AU, 0, A_WIDTH), (COL_AG, 512, A_WIDTH),
                  (COL_BG, W_IN_KROPE + B_ROPE, N_SMALL - COL_BG))


def _cast_rows_body(w_ref, o_ref):
    o_ref[...] = w_ref[...].astype(BF16)


def _cast_small_body(src_ref, w_ref, o_ref):
    del src_ref
    is_kpe = pl.program_id(1) == COL_KPE // CAST_ROWS

    @pl.when(jnp.logical_not(is_kpe))
    def _():
        o_ref[...] = w_ref[...].astype(BF16)

    @pl.when(is_kpe)
    def _():
        o_ref[...] = jnp.zeros(o_ref.shape, BF16)
        o_ref[ROPE_LO:ROPE_LO + ROPE_HALF, :] = w_ref[:ROPE_HALF, :].astype(BF16)
        o_ref[ROPE_HI:ROPE_HI + ROPE_HALF, :] = w_ref[ROPE_HALF:B_ROPE, :].astype(BF16)


def _cast_in_weights(w_in):
    depth, d_model, in_width = w_in.shape
    w_t = jnp.swapaxes(w_in, 1, 2).reshape(depth * in_width, d_model)
    n_merge = N_BRANCH * D_MODEL
    n_blk = n_merge // MERGE_CAST_ROWS
    w_merge_t = pl.pallas_call(
        _cast_rows_body,
        out_shape=jax.ShapeDtypeStruct((depth * n_merge, d_model), BF16),
        grid=(depth, n_blk),
        in_specs=[pl.BlockSpec(
            (pl.Element(MERGE_CAST_ROWS), pl.Element(d_model)),
            lambda l, r: ((l * (in_width // B_ROPE) + W_IN_MERGE // B_ROPE
                           + r * (MERGE_CAST_ROWS // B_ROPE)) * B_ROPE, 0))],
        out_specs=pl.BlockSpec((MERGE_CAST_ROWS, d_model), lambda l, r: (l * n_blk + r, 0)),
        compiler_params=pltpu.CompilerParams(dimension_semantics=("parallel", "parallel")),
        name="cast_merge_weights",
    )(w_t)

    n_blk = N_SMALL // CAST_ROWS
    src_rows = [0] * n_blk
    for dst, src, rows in SMALL_SEGMENTS:
        for b in range(rows // CAST_ROWS):
            src_rows[dst // CAST_ROWS + b] = src + b * CAST_ROWS
    src_rows[COL_KPE // CAST_ROWS] = W_IN_KROPE
    w_small_t = pl.pallas_call(
        _cast_small_body,
        out_shape=jax.ShapeDtypeStruct((depth * N_SMALL, d_model), BF16),
        grid_spec=pltpu.PrefetchScalarGridSpec(
            num_scalar_prefetch=1,
            grid=(depth, n_blk),
            in_specs=[pl.BlockSpec(
                (pl.Element(CAST_ROWS), pl.Element(d_model)),
                lambda l, r, src: ((l * (in_width // B_ROPE) + src[r]) * B_ROPE, 0))],
            out_specs=pl.BlockSpec((CAST_ROWS, d_model), lambda l, r, src: (l * n_blk + r, 0))),
        compiler_params=pltpu.CompilerParams(dimension_semantics=("parallel", "parallel")),
        name="cast_branch_weights",
    )(jnp.asarray(src_rows, jnp.int32) // B_ROPE, w_t)
    return (w_small_t.reshape(depth, N_SMALL, d_model),
            w_merge_t.reshape(depth, n_merge, d_model))


def _row(v):
    return v.reshape(1, -1).astype(F32)


def _prep_layer(norm_g, b_merge, a_norm_g, a_w_s, a_b_s, b_q_norm_g, b_kv_norm_g,
                b_w_uq, b_w_ukv, b_qk_g_q, b_qk_g_k, c_a_re, c_a_im, c_log_dt, c_b_re,
                c_b_im, c_c_re, c_c_im, c_d, c_w_glu, c_b_glu, m_norm_g, m_w_kv, m_qk_g_q,
                m_qk_g_k, w_br_a, w_br_b, w_br_c, w_br_m, w_out):
    p = {}
    p["norm_g"] = _row(norm_g)
    p["b_merge"] = _row(b_merge)

    p["a_norm_g"] = _row(a_norm_g)
    causal = jnp.tril(jnp.ones((A_CHUNK, A_CHUNK), dtype=bool))
    p["a_wst"] = jnp.where(causal, a_w_s, 0.0).reshape(A_GROUPS * A_CHUNK, A_CHUNK).astype(BF16)
    p["a_bias"] = jnp.repeat(a_b_s.T, A_GROUP_DIM, axis=1).astype(F32)

    p["b_q_norm_g"] = _row(b_q_norm_g)
    p["b_kv_norm_g"] = _row(b_kv_norm_g)
    wuq = _head_layout(b_w_uq.reshape(B_Q_RANK, B_HEADS, B_QK_DIM))
    p["b_wuqt"] = wuq.reshape(B_Q_RANK, QK_PAD).T.astype(BF16)
    wukv = b_w_ukv.reshape(B_KV_RANK, B_HEADS, B_NOPE + B_VDIM)
    wuk = _head_layout(jnp.pad(wukv[:, :, :B_NOPE], ((0, 0), (0, 0), (0, B_ROPE))))
    p["b_wuk"] = wuk.reshape(B_KV_RANK, QK_PAD).astype(BF16)
    wuv = jnp.pad(wukv[:, :, B_NOPE:], ((0, 0), (0, 0), (0, V_AUG - B_VDIM)))
    p["b_wuvt"] = wuv.reshape(B_KV_RANK, VT_ROWS).T.astype(BF16)
    p["b_gq"] = jnp.broadcast_to(_head_layout(b_qk_g_q.astype(F32))[:, None],
                                 (HEAD_PAD, TOKEN_TILE))
    p["b_gk"] = _row(_head_layout(b_qk_g_k))

    def rep(a):
        return jnp.repeat(a.astype(F32), C_GROUP, axis=0)

    def gcp(b):
        return b.astype(F32).transpose(0, 2, 1).reshape(C_WIDTH, C_STATE)

    log_dt = jnp.broadcast_to(c_log_dt.astype(F32)[:, None], (C_GROUPS, C_STATE))
    abar_re, abar_im, bbar_re, bbar_im = _s5_discretise(
        rep(c_a_re), rep(c_a_im), rep(log_dt), gcp(c_b_re), gcp(c_b_im))
    eye = jnp.eye(C_GROUPS, dtype=F32)

    def expand_in(m):
        m = m.reshape(C_GROUPS, C_GROUP, C_STATE)
        return (eye[:, None, :, None] * m[:, :, None, :]).reshape(C_WIDTH, C_STATES)

    def expand_out(m):
        return (eye[:, None, :, None] * m.transpose(0, 2, 1)[:, :, None, :]).reshape(
            C_STATES, C_WIDTH)

    n_lb = C_STATES // S5_LANE_BLOCK
    bmat = jnp.stack([expand_in(bbar_re).reshape(C_WIDTH, n_lb, S5_LANE_BLOCK),
                      expand_in(bbar_im).reshape(C_WIDTH, n_lb, S5_LANE_BLOCK)], axis=2)
    p["c_bmat"] = bmat.reshape(C_WIDTH, 2 * C_STATES).astype(BF16)
    cmat = jnp.stack([expand_out(c_c_re.astype(F32)).reshape(n_lb, S5_LANE_BLOCK, C_WIDTH),
                      -expand_out(c_c_im.astype(F32)).reshape(n_lb, S5_LANE_BLOCK, C_WIDTH)],
                     axis=1)
    p["c_cmat"] = cmat.reshape(2 * C_STATES, C_WIDTH).astype(BF16)
    p["c_are"] = _row(abar_re.reshape(C_GROUPS, C_GROUP, C_STATE)[:, 0, :])
    p["c_aim"] = _row(abar_im.reshape(C_GROUPS, C_GROUP, C_STATE)[:, 0, :])
    p["c_d"] = _row(c_d)
    p["c_w_glu"] = c_w_glu.astype(BF16)
    p["c_b_glu"] = _row(c_b_glu)

    p["m_norm_g"] = _row(m_norm_g)
    p["m_w_kv"] = m_w_kv.astype(BF16)
    p["m_gq"] = _row(jnp.tile(m_qk_g_q, M_HEADS)) * (M_HEAD_DIM ** -0.5)
    p["m_gk"] = _row(jnp.tile(m_qk_g_k, M_HEADS))
    p["w_br_a"] = w_br_a.astype(BF16)
    p["w_br_b"] = w_br_b.astype(BF16)
    p["w_br_c"] = w_br_c.astype(BF16)
    p["w_br_m"] = w_br_m.astype(BF16)
    p["w_out"] = w_out.astype(BF16)
    return p


def kernel(x, mem, positions, norm_g, w_in, b_merge, a_norm_g, a_w_s, a_b_s, b_q_norm_g, b_kv_norm_g, b_w_uq, b_w_ukv, b_qk_g_q, b_qk_g_k, c_a_re, c_a_im, c_log_dt, c_b_re, c_b_im, c_c_re, c_c_im, c_d, c_w_glu, c_b_glu, m_norm_g, m_w_kv, m_qk_g_q, m_qk_g_k, w_br_a, w_br_b, w_br_c, w_br_m, w_out):
    bsz, seq, d_model = x.shape
    mem_len = mem.shape[1]
    depth = norm_g.shape[0]
    assert d_model == D_MODEL and seq % (SUB_TILES * TOKEN_TILE) == 0
    assert seq % S5_CHUNK == 0 and bsz == 8 and TOKEN_TILE == ATTN_TILE

    stacked = (norm_g, b_merge, a_norm_g, a_w_s, a_b_s, b_q_norm_g, b_kv_norm_g,
               b_w_uq, b_w_ukv, b_qk_g_q, b_qk_g_k, c_a_re, c_a_im, c_log_dt, c_b_re,
               c_b_im, c_c_re, c_c_im, c_d, c_w_glu, c_b_glu, m_norm_g, m_w_kv, m_qk_g_q,
               m_qk_g_k, w_br_a, w_br_b, w_br_c, w_br_m, w_out)

    tabs = _rope_tables(positions)
    x2 = x.reshape(bsz * seq, d_model)
    mem2 = mem.reshape(bsz * mem_len, d_model)
    p = jax.vmap(_prep_layer)(*stacked)
    p["w_small"], p["w_merge"] = _cast_in_weights(w_in)
    mk, mv = _memkv(mem2, p, bsz, mem_len)
    for layer in range(depth):
        ya, q, k, vt, bg, ci, cg, mq, mg = _inproj(x2, p, layer, tabs, bsz, seq)
        yb = _attention(q, k, vt, bg, bsz, seq)
        yc = _s5(ci, cg, p, layer, bsz, seq)
        x2 = _merge(x2, ya, yb, yc, mq, mg, mk, mv, p, layer, bsz, seq, mem_len)
    return x2.reshape(bsz, seq, d_model)
```

```python
import functools
import math

import jax
import jax.numpy as jnp
from jax import lax
from jax.experimental import pallas as pl
from jax.experimental.pallas import tpu as pltpu

F32 = jnp.float32
BF16 = jnp.bfloat16

D_MODEL = 1024
EPS = 1e-6
N_BRANCH = 4

A_WIDTH = 256
A_GROUPS = 4
A_GROUP_DIM = A_WIDTH // A_GROUPS
A_CHUNK = 128

B_HEADS = 8
B_NOPE = 64
B_ROPE = 32
B_QK_DIM = B_NOPE + B_ROPE
B_VDIM = 64
B_Q_RANK = 768
B_KV_RANK = 256
B_WIDTH = B_HEADS * B_VDIM
ROPE_THETA = 10000.0
HEAD_PAD = 128
QK_PAD = B_HEADS * HEAD_PAD
V_AUG = B_VDIM + 16
VT_ROWS = B_HEADS * V_AUG

C_WIDTH = 256
C_GROUP = 16
C_GROUPS = C_WIDTH // C_GROUP
C_STATE = 64
C_STATES = C_GROUPS * C_STATE

M_HEADS = 4
M_HEAD_DIM = 64
M_WIDTH = M_HEADS * M_HEAD_DIM

COL_CKV = 0
COL_KPE = 256
COL_CQ = 384
COL_AV = 1152
COL_AU = 1408
COL_AG = 1664
COL_BG = 1920
COL_CI = 2432
COL_CG = 2688
COL_MQ = 2944
COL_MG = 3200
N_SMALL = 3456

V7X_VMEM_LIMIT_BYTES = 56 * 1024 * 1024

TOKEN_TILE = 256
SUB_TILES = 2
ATTN_TILE = 256
S5_CHUNK = 64
S5_LANE_BLOCK = 256


def _sigmoid(x):
    return 1.0 / (1.0 + jnp.exp(-x))


def _silu(x):
    return x * _sigmoid(x)


def _rms(x, g):
    return x * lax.rsqrt(jnp.mean(x * x, axis=-1, keepdims=True) + EPS) * g


def _dot(a, b):
    return jnp.dot(a, b, preferred_element_type=F32)


def _dot_nt(a, b):
    return lax.dot_general(a, b, (((1,), (1,)), ((), ())), preferred_element_type=F32)


def _layer_spec(a, layer, **kwargs):
    zeros = (0,) * (a.ndim - 1)
    return pl.BlockSpec((None,) + a.shape[1:], lambda *_: (layer,) + zeros, **kwargs)


ROPE_HALF = B_ROPE // 2
ROPE_LO = 0
ROPE_HI = HEAD_PAD // 2


def _head_layout(w):
    nope, rope = w[..., :B_NOPE], w[..., B_NOPE:]
    split = ROPE_HI - ROPE_HALF
    pad = jnp.zeros(w.shape[:-1] + (HEAD_PAD - B_QK_DIM,), w.dtype)
    return jnp.concatenate([rope[..., :ROPE_HALF], nope[..., :split], rope[..., ROPE_HALF:],
                            nope[..., split:], pad], axis=-1)


def _rope_table_body(pos_row_ref, freq_col_ref, c_ref, s_ref, ct_ref, st_ref):
    ang_t = freq_col_ref[...] * pos_row_ref[...]
    cos_t = jnp.cos(ang_t)
    sin_t = jnp.sin(ang_t)
    ct_ref[...] = cos_t
    st_ref[...] = sin_t
    gap = ROPE_HI - ROPE_HALF
    ones = jnp.ones((gap, ang_t.shape[1]), F32)
    zeros = jnp.zeros((gap, ang_t.shape[1]), F32)
    c_ref[...] = jnp.concatenate([cos_t, ones, cos_t, ones], axis=0).T
    s_ref[...] = jnp.concatenate([-sin_t, zeros, sin_t, zeros], axis=0).T


def _rope_tables(positions):
    n_tok = positions.size
    inv_freq = ROPE_THETA ** (-jnp.arange(ROPE_HALF, dtype=F32) / ROPE_HALF)
    tm = 1024
    freq_col = jnp.broadcast_to(inv_freq[:, None], (ROPE_HALF, tm))
    pos = positions.astype(F32)
    tab = jax.ShapeDtypeStruct((n_tok, HEAD_PAD), F32)
    tab_t = jax.ShapeDtypeStruct((ROPE_HALF, n_tok), F32)
    return pl.pallas_call(
        _rope_table_body,
        out_shape=(tab, tab, tab_t, tab_t),
        grid=(n_tok // tm,),
        in_specs=[pl.BlockSpec((1, tm), lambda i: (0, i)),
                  pl.BlockSpec((ROPE_HALF, tm), lambda i: (0, 0))],
        out_specs=(pl.BlockSpec((tm, HEAD_PAD), lambda i: (i, 0)),) * 2
        + (pl.BlockSpec((ROPE_HALF, tm), lambda i: (0, i)),) * 2,
        compiler_params=pltpu.CompilerParams(dimension_semantics=("parallel",)),
        name="rope_tables",
    )(pos.reshape(1, n_tok), freq_col)


def _inproj_body(x_ref, ng_ref, w_ref, ang_ref, wst_ref, bias_ref, qng_ref, kvng_ref,
                 wuqt_ref, wuk_ref, wuvt_ref, gq_ref, gk_ref, c_ref, s_ref, ct_ref, st_ref,
                 ya_ref, qt_ref, k_ref, vt_ref, bg_ref, ci_ref, cg_ref, mq_ref, mg_ref):
    tm = x_ref.shape[0]
    h = _rms(x_ref[...], ng_ref[...]).astype(BF16)

    z_lat = _dot_nt(h, w_ref[:COL_AV, :])

    def lat(lo, width):
        return z_lat[:, lo:lo + width]

    ckv = _rms(lat(COL_CKV, B_KV_RANK), kvng_ref[...]).astype(BF16)
    kf = _dot(ckv, wuk_ref[...])
    z_rest = _dot_nt(h, w_ref[COL_AV:, :])

    def proj(lo, width):
        return z_rest[:, lo - COL_AV:lo - COL_AV + width]

    kpe = lat(COL_KPE, HEAD_PAD)
    c_tab, s_tab = c_ref[...], s_ref[...]
    for hd in range(B_HEADS):
        lanes = slice(hd * HEAD_PAD, (hd + 1) * HEAD_PAD)
        kh = kf[:, lanes] + kpe
        ssq = jnp.sum(kh * kh, axis=-1, keepdims=True)
        kh = kh * lax.rsqrt(ssq * (1.0 / B_QK_DIM) + EPS) * gk_ref[...]
        k_ref[:, lanes] = (kh * c_tab + pltpu.roll(kh, HEAD_PAD // 2, 1) * s_tab).astype(BF16)
    vt = _dot_nt(wuvt_ref[...], ckv)
    ones_row = lax.broadcasted_iota(jnp.int32, vt.shape, 0) % V_AUG == B_VDIM
    vt_ref[...] = jnp.where(ones_row, 1.0, vt).astype(BF16)

    u = jax.nn.gelu(proj(COL_AU, A_WIDTH))
    v = _rms(jax.nn.gelu(proj(COL_AV, A_WIDTH)), ang_ref[...]).astype(BF16)
    ag = _silu(proj(COL_AG, A_WIDTH))
    lane_group = lax.broadcasted_iota(jnp.int32, (A_CHUNK, A_WIDTH), 1) // A_GROUP_DIM
    for c in range(tm // A_CHUNK):
        rows = slice(c * A_CHUNK, (c + 1) * A_CHUNK)
        s_all = _dot(wst_ref[...], v[rows, :])
        s = s_all[(A_GROUPS - 1) * A_CHUNK:, :]
        for g in range(A_GROUPS - 2, -1, -1):
            s = jnp.where(lane_group == g, s_all[g * A_CHUNK:(g + 1) * A_CHUNK, :], s)
        s = s + bias_ref[...]
        ya_ref[rows, :] = (u[rows, :] * s * ag[rows, :]).astype(BF16)

    q_scale = B_QK_DIM ** -0.5 * math.log2(math.e)
    cq = _rms(lat(COL_CQ, B_Q_RANK), qng_ref[...]).astype(BF16)
    half_rows = QK_PAD // 2
    qt_halves = [_dot_nt(wuqt_ref[i * half_rows:(i + 1) * half_rows, :], cq) for i in range(2)]
    cos_t, sin_t = ct_ref[...], st_ref[...]
    for hd in range(B_HEADS):
        r0 = hd * HEAD_PAD
        qr = r0 % half_rows
        qh = qt_halves[r0 // half_rows][qr:qr + HEAD_PAD, :]
        ssq = jnp.sum(qh * qh, axis=0, keepdims=True)
        qh = qh * (lax.rsqrt(ssq * (1.0 / B_QK_DIM) + EPS) * q_scale) * gq_ref[...]
        x1 = qh[ROPE_LO:ROPE_LO + ROPE_HALF, :]
        x2 = qh[ROPE_HI:ROPE_HI + ROPE_HALF, :]
        qt_ref[r0:r0 + HEAD_PAD, :] = qh.astype(BF16)
        qt_ref[r0 + ROPE_LO:r0 + ROPE_LO + ROPE_HALF, :] = (x1 * cos_t - x2 * sin_t).astype(BF16)
        qt_ref[r0 + ROPE_HI:r0 + ROPE_HI + ROPE_HALF, :] = (x2 * cos_t + x1 * sin_t).astype(BF16)

    bg_ref[...] = _silu(proj(COL_BG, B_WIDTH))
    ci_ref[...] = proj(COL_CI, C_WIDTH)
    cg_ref[...] = _silu(proj(COL_CG, C_WIDTH))
    mq_ref[...] = proj(COL_MQ, M_WIDTH)
    mg_ref[...] = _silu(proj(COL_MG, M_WIDTH))


def _inproj_step(x_ref, *refs):
    weights = refs[:12]
    c_ref, s_ref, ct_ref, st_ref = refs[12:16]
    ya_ref, qt_ref, k_ref, vt_ref, bg_ref, ci_ref, cg_ref, mq_ref, mg_ref = refs[16:]
    tm = TOKEN_TILE
    for sub in range(SUB_TILES):
        rows = pl.ds(sub * tm, tm)
        _inproj_body(x_ref.at[rows], *weights, c_ref.at[rows], s_ref.at[rows],
                     ct_ref.at[:, rows], st_ref.at[:, rows],
                     ya_ref.at[rows], qt_ref.at[sub], k_ref.at[rows], vt_ref.at[sub],
                     bg_ref.at[rows], ci_ref.at[rows], cg_ref.at[rows], mq_ref.at[rows],
                     mg_ref.at[rows])


def _inproj(x2, p, layer, tabs, bsz, seq):
    n_tok = x2.shape[0]
    tm = TOKEN_TILE
    ts = SUB_TILES * tm
    nst = seq // ts

    def tok(i):
        return (i, 0)

    def seq_major(i):
        return (i % nst, i // nst)

    weights = (p["norm_g"], p["w_small"], p["a_norm_g"], p["a_wst"], p["a_bias"],
               p["b_q_norm_g"], p["b_kv_norm_g"], p["b_wuqt"], p["b_wuk"], p["b_wuvt"],
               p["b_gq"], p["b_gk"])
    in_specs = ([pl.BlockSpec((ts, D_MODEL), tok)] + [_layer_spec(w, layer) for w in weights]
                + [pl.BlockSpec((ts, HEAD_PAD), tok)] * 2
                + [pl.BlockSpec((ROPE_HALF, ts), lambda i: (0, i))] * 2)
    out_shape = (
        jax.ShapeDtypeStruct((n_tok, A_WIDTH), BF16),
        jax.ShapeDtypeStruct((n_tok // tm, QK_PAD, tm), BF16),
        jax.ShapeDtypeStruct((n_tok, QK_PAD), BF16),
        jax.ShapeDtypeStruct((n_tok // tm, VT_ROWS, tm), BF16),
        jax.ShapeDtypeStruct((n_tok, B_WIDTH), F32),
        jax.ShapeDtypeStruct((seq, bsz * C_WIDTH), F32),
        jax.ShapeDtypeStruct((seq, bsz * C_WIDTH), F32),
        jax.ShapeDtypeStruct((n_tok, M_WIDTH), F32),
        jax.ShapeDtypeStruct((n_tok, M_WIDTH), F32),
    )
    out_specs = (
        pl.BlockSpec((ts, A_WIDTH), tok),
        pl.BlockSpec((SUB_TILES, QK_PAD, tm), lambda i: (i, 0, 0)),
        pl.BlockSpec((ts, QK_PAD), tok),
        pl.BlockSpec((SUB_TILES, VT_ROWS, tm), lambda i: (i, 0, 0)),
        pl.BlockSpec((ts, B_WIDTH), tok),
        pl.BlockSpec((ts, C_WIDTH), seq_major),
        pl.BlockSpec((ts, C_WIDTH), seq_major),
        pl.BlockSpec((ts, M_WIDTH), tok),
        pl.BlockSpec((ts, M_WIDTH), tok),
    )
    return pl.pallas_call(
        _inproj_step,
        out_shape=out_shape,
        grid=(n_tok // ts,),
        in_specs=in_specs,
        out_specs=out_specs,
        compiler_params=pltpu.CompilerParams(
            dimension_semantics=("parallel",), vmem_limit_bytes=V7X_VMEM_LIMIT_BYTES),
        name="inproj",
    )(x2, *weights, *tabs)


def _attn_body(qt_ref, k_ref, vt_ref, g_ref, o_ref, m_ref, a_ref, acc_ref, ot_ref, s_ref):
    tq = qt_ref.shape[1]
    qi = pl.program_id(1)
    key_le_query = (lax.broadcasted_iota(jnp.int32, (tq, tq), 0)
                    <= lax.broadcasted_iota(jnp.int32, (tq, tq), 1))
    m_ref[1] = jnp.full(m_ref.shape[1:], -jnp.inf, F32)
    acc_ref[...] = jnp.zeros(acc_ref.shape, F32)

    def scores_head(j, masked, slot, hd):
        r0 = pl.multiple_of(j * tq, tq)
        lanes = slice(hd * HEAD_PAD, (hd + 1) * HEAD_PAD)
        st = _dot(k_ref[pl.ds(r0, tq), lanes], qt_ref[lanes, :])
        if masked:
            st = jnp.where(key_le_query, st, -jnp.inf)
        s_ref[slot, hd] = st
        m_old = m_ref[1 - slot, hd:hd + 1, :]
        m_new = jnp.maximum(m_old, jnp.max(st, axis=0, keepdims=True))
        m_ref[slot, hd:hd + 1, :] = m_new
        a_ref[slot, hd:hd + 1, :] = jnp.exp2(m_old - m_new)

    def accumulate_head(j, slot, hd):
        rows = slice(hd * V_AUG, (hd + 1) * V_AUG)
        pt = jnp.exp2(s_ref[slot, hd] - m_ref[slot, hd:hd + 1, :])
        acc_ref[rows, :] = (a_ref[slot, hd:hd + 1, :] * acc_ref[rows, :]
                            + _dot(vt_ref[j, rows, :], pt.astype(BF16)))

    def scores(j, masked, slot):
        for hd in range(B_HEADS):
            scores_head(j, masked, slot, hd)

    def accumulate(j, slot):
        for hd in range(B_HEADS):
            accumulate_head(j, slot, hd)

    def scores_and_accumulate(j, masked, slot):
        for hd in range(B_HEADS):
            scores_head(j, masked, slot, hd)
            accumulate_head(j - 1, 1 - slot, hd)

    @pl.when(qi == 0)
    def _():
        scores(0, True, 0)
        accumulate(0, 0)

    @pl.when(qi > 0)
    def _():
        scores(0, False, 0)

        def two_blocks(i, carry):
            scores_and_accumulate(2 * i + 1, False, 1)
            scores_and_accumulate(2 * i + 2, False, 0)
            return carry

        lax.fori_loop(0, (qi - 1) // 2, two_blocks, 0)

        @pl.when(qi % 2 == 0)
        def _():
            scores_and_accumulate(qi - 1, False, 1)
            scores_and_accumulate(qi, True, 0)
            accumulate(qi, 0)

        @pl.when(qi % 2 == 1)
        def _():
            scores_and_accumulate(qi, True, 1)
            accumulate(qi, 1)

    for hd in range(B_HEADS):
        r0 = hd * V_AUG
        denom = acc_ref[r0 + B_VDIM:r0 + B_VDIM + 1, :]
        ot_ref[hd * B_VDIM:(hd + 1) * B_VDIM, :] = acc_ref[r0:r0 + B_VDIM, :] * (1.0 / denom)
    o_ref[...] = (ot_ref[...].T * g_ref[...]).astype(BF16)


def _attention(q, k, vt, gate, bsz, seq):
    tq = ATTN_TILE
    nq = seq // tq
    return pl.pallas_call(
        _attn_body,
        out_shape=jax.ShapeDtypeStruct((bsz * seq, B_WIDTH), BF16),
        grid=(bsz, nq),
        in_specs=[pl.BlockSpec((None, QK_PAD, tq), lambda b, i: (b * nq + i, 0, 0)),
                  pl.BlockSpec((seq, QK_PAD), lambda b, i: (b, 0)),
                  pl.BlockSpec((nq, VT_ROWS, tq), lambda b, i: (b, 0, 0)),
                  pl.BlockSpec((tq, B_WIDTH), lambda b, i: (b * nq + i, 0))],
        out_specs=pl.BlockSpec((tq, B_WIDTH), lambda b, i: (b * nq + i, 0)),
        scratch_shapes=[pltpu.VMEM((2, B_HEADS, tq), F32), pltpu.VMEM((2, B_HEADS, tq), F32),
                        pltpu.VMEM((VT_ROWS, tq), F32), pltpu.VMEM((B_WIDTH, tq), F32),
                        pltpu.VMEM((2, B_HEADS, tq, tq), F32)],
        compiler_params=pltpu.CompilerParams(
            dimension_semantics=("parallel", "parallel"),
            vmem_limit_bytes=V7X_VMEM_LIMIT_BYTES),
        name="latent_attention",
    )(q, k, vt, gate)


def _s5_discretise_body(are_ref, aim_ref, ldt_ref, bre_ref, bim_ref,
                        abar_re_ref, abar_im_ref, bbar_re_ref, bbar_im_ref):
    a_re, a_im = are_ref[...], aim_ref[...]
    dt = jnp.exp(ldt_ref[...])
    mag = jnp.exp(a_re * dt)
    abar_re = mag * jnp.cos(a_im * dt)
    abar_im = mag * jnp.sin(a_im * dt)
    num_re = abar_re - 1.0
    inv_den = 1.0 / (a_re * a_re + a_im * a_im)
    q_re = (num_re * a_re + abar_im * a_im) * inv_den
    q_im = (abar_im * a_re - num_re * a_im) * inv_den
    b_re, b_im = bre_ref[...], bim_ref[...]
    abar_re_ref[...] = abar_re
    abar_im_ref[...] = abar_im
    bbar_re_ref[...] = q_re * b_re - q_im * b_im
    bbar_im_ref[...] = q_re * b_im + q_im * b_re


def _s5_discretise(a_re, a_im, log_dt, b_re, b_im):
    out = jax.ShapeDtypeStruct(a_re.shape, F32)
    return pl.pallas_call(_s5_discretise_body, out_shape=(out,) * 4, name="s5_discretise")(
        a_re, a_im, log_dt, b_re, b_im)


def _s5_body(u_ref, cg_ref, bmat_ref, cmat_ref, are_ref, aim_ref, d_ref, wglu_ref, bglu_ref,
             y_ref, state_ref, buf_ref):
    lt, bsz, width = u_ref.shape
    rows = lt * bsz
    lb_w = S5_LANE_BLOCK

    @pl.when(pl.program_id(0) == 0)
    def _():
        state_ref[...] = jnp.zeros_like(state_ref)

    u2 = u_ref[...].reshape(rows, width)
    ub = u2.astype(BF16)
    half = rows // 2
    y_halves = [None, None]
    for lb in range(C_STATES // lb_w):
        cols = slice(2 * lb * lb_w, 2 * (lb + 1) * lb_w)
        re_l = slice(2 * lb * lb_w, (2 * lb + 1) * lb_w)
        im_l = slice((2 * lb + 1) * lb_w, 2 * (lb + 1) * lb_w)
        buf_ref[:, cols] = _dot(ub, bmat_ref[:, cols])
        a_re = jnp.broadcast_to(are_ref[:, lb * lb_w:(lb + 1) * lb_w], (bsz, lb_w))
        a_im = jnp.broadcast_to(aim_ref[:, lb * lb_w:(lb + 1) * lb_w], (bsz, lb_w))
        s_re, s_im = state_ref[:, re_l], state_ref[:, im_l]
        for t in range(lt):
            r = slice(t * bsz, (t + 1) * bsz)
            n_re = a_re * s_re - a_im * s_im + buf_ref[r, re_l]
            n_im = a_re * s_im + a_im * s_re + buf_ref[r, im_l]
            buf_ref[r, re_l] = n_re
            buf_ref[r, im_l] = n_im
            s_re, s_im = n_re, n_im
        state_ref[:, re_l] = s_re
        state_ref[:, im_l] = s_im
        for hf in range(2):
            r = slice(hf * half, (hf + 1) * half)
            part = _dot(buf_ref[r, cols].astype(BF16), cmat_ref[cols, :])
            y_halves[hf] = part if y_halves[hf] is None else y_halves[hf] + part

    y = jnp.concatenate(y_halves, axis=0) + d_ref[...] * u2
    y = jax.nn.gelu(y)
    y = y * _sigmoid(_dot(y.astype(BF16), wglu_ref[...]) + bglu_ref[...])
    y_ref[...] = (y * cg_ref[...].reshape(rows, width)).reshape(lt, bsz, width)


def _s5(c_in, c_gate, p, layer, bsz, seq):
    lt = S5_CHUNK
    u3 = c_in.reshape(seq, bsz, C_WIDTH)
    g3 = c_gate.reshape(seq, bsz, C_WIDTH)

    def chunk(i):
        return (i, 0, 0)

    weights = (p["c_bmat"], p["c_cmat"], p["c_are"], p["c_aim"], p["c_d"], p["c_w_glu"],
               p["c_b_glu"])
    y = pl.pallas_call(
        _s5_body,
        out_shape=jax.ShapeDtypeStruct((seq, bsz, C_WIDTH), F32),
        grid=(seq // lt,),
        in_specs=[pl.BlockSpec((lt, bsz, C_WIDTH), chunk)] * 2
        + [_layer_spec(w, layer) for w in weights],
        out_specs=pl.BlockSpec((lt, bsz, C_WIDTH), chunk),
        scratch_shapes=[pltpu.VMEM((bsz, 2 * C_STATES), F32),
                        pltpu.VMEM((lt * bsz, 2 * C_STATES), F32)],
        compiler_params=pltpu.CompilerParams(
            dimension_semantics=("arbitrary",), vmem_limit_bytes=V7X_VMEM_LIMIT_BYTES),
        name="s5_scan",
    )(u3, g3, *weights)
    return y.reshape(seq, bsz * C_WIDTH)


def _head_rms(x, gain, n_heads, head_dim):
    head_id = lax.broadcasted_iota(jnp.int32, x.shape, 1) // head_dim
    out = jnp.zeros_like(x)
    for hd in range(n_heads):
        xh = jnp.where(head_id == hd, x, 0.0)
        ssq = jnp.sum(xh * xh, axis=-1, keepdims=True)
        out = out + xh * lax.rsqrt(ssq * (1.0 / head_dim) + EPS)
    return out * gain


def _memkv_body(mem_ref, ng_ref, wkv_ref, gk_ref, k_ref, v_ref):
    mem_len = mem_ref.shape[0]
    mem = mem_ref[...]
    head_id = lax.broadcasted_iota(jnp.int32, (mem_len, M_WIDTH), 1) // M_HEAD_DIM
    for layer in range(ng_ref.shape[0]):
        mh = _rms(mem, ng_ref[layer]).astype(BF16)
        kv = _dot(mh, wkv_ref[layer])
        k = _head_rms(kv[:, :M_WIDTH], gk_ref[layer], M_HEADS, M_HEAD_DIM)
        v = kv[:, M_WIDTH:]
        for hd in range(M_HEADS):
            rows = slice(hd * mem_len, (hd + 1) * mem_len)
            k_ref[layer, rows, :] = jnp.where(head_id == hd, k, 0.0).astype(BF16)
            v_ref[layer, rows, :] = jnp.where(head_id == hd, v, 0.0).astype(BF16)


def _memkv(mem2, p, bsz, mem_len):
    weights = (p["m_norm_g"], p["m_w_kv"], p["m_gk"])
    depth = weights[0].shape[0]
    out = jax.ShapeDtypeStruct((depth, bsz * M_HEADS * mem_len, M_WIDTH), BF16)
    return pl.pallas_call(
        _memkv_body,
        out_shape=(out, out),
        grid=(bsz,),
        in_specs=[pl.BlockSpec((mem_len, D_MODEL), lambda b: (b, 0))]
        + [pl.BlockSpec(w.shape, lambda b: (0, 0, 0)) for w in weights],
        out_specs=(pl.BlockSpec((depth, M_HEADS * mem_len, M_WIDTH), lambda b: (0, b, 0)),) * 2,
        compiler_params=pltpu.CompilerParams(dimension_semantics=("parallel",)),
        name="memory_kv",
    )(mem2, *weights)


def _merge_body(x_ref, ng_ref, wm_ref, bm_ref, ya_ref, yb_ref, yc_ref, mq_ref, mg_ref,
                mk_ref, mv_ref, gq_ref, wa_ref, wb_ref, wc_ref, wmm_ref, wo_ref, o_ref):
    x = x_ref[...]
    h = _rms(x, ng_ref[...]).astype(BF16)

    def gated(br, y, w_ref):
        cols = slice(br * D_MODEL, (br + 1) * D_MODEL)
        gate = _sigmoid(_dot_nt(h, wm_ref[cols, :]) + bm_ref[:, cols])
        return gate * _dot(y, w_ref[...])

    mq = mq_ref[...]
    mem_len = mk_ref.shape[0] // M_HEADS
    same_head = (lax.broadcasted_iota(jnp.int32, (M_WIDTH, M_WIDTH), 0) // M_HEAD_DIM
                 == lax.broadcasted_iota(jnp.int32, (M_WIDTH, M_WIDTH), 1) // M_HEAD_DIM)
    head_ones = jnp.where(same_head, 1.0, 0.0).astype(BF16)
    q_sq = mq * mq
    q_sq_hi = q_sq.astype(BF16)
    q_sq_lo = (q_sq - q_sq_hi.astype(F32)).astype(BF16)
    ssq = _dot(q_sq_hi, head_ones) + _dot(q_sq_lo, head_ones)
    merged = gated(0, ya_ref[...], wa_ref)
    qn = (mq * lax.rsqrt(ssq * (1.0 / M_HEAD_DIM) + EPS) * gq_ref[...]).astype(BF16)
    s_all = _dot_nt(qn, mk_ref[...])
    merged = merged + gated(1, yb_ref[...], wb_ref)
    probs = []
    for hd in range(M_HEADS):
        s = s_all[:, hd * mem_len:(hd + 1) * mem_len]
        e = jnp.exp(s - jnp.max(s, axis=-1, keepdims=True))
        probs.append((e * (1.0 / jnp.sum(e, axis=-1, keepdims=True))).astype(BF16))
    om = _dot(jnp.concatenate(probs, axis=1), mv_ref[...])
    merged = merged + gated(2, yc_ref[...].astype(BF16), wc_ref)
    ym = (om * mg_ref[...]).astype(BF16)
    merged = merged + gated(3, ym, wmm_ref)
    o_ref[...] = x + _dot(merged.astype(BF16), wo_ref[...])


def _merge_step(x_ref, ng_ref, wm_ref, bm_ref, ya_ref, yb_ref, yc_ref, mq_ref, mg_ref,
                mk_ref, mv_ref, gq_ref, wa_ref, wb_ref, wc_ref, wmm_ref, wo_ref, o_ref):
    tm = TOKEN_TILE
    for sub in range(SUB_TILES):
        rows = pl.ds(sub * tm, tm)
        _merge_body(x_ref.at[rows], ng_ref, wm_ref, bm_ref, ya_ref.at[rows], yb_ref.at[rows],
                    yc_ref.at[rows], mq_ref.at[rows], mg_ref.at[rows], mk_ref, mv_ref, gq_ref,
                    wa_ref, wb_ref, wc_ref, wmm_ref, wo_ref, o_ref.at[rows])


def _merge(x2, ya, yb, yc, mq, mg, mk, mv, p, layer, bsz, seq, mem_len):
    n_tok = x2.shape[0]
    ts = SUB_TILES * TOKEN_TILE
    nst = seq // ts

    def tok(i):
        return (i, 0)

    def seq_major(i):
        return (i % nst, i // nst)

    def per_batch(i):
        return (layer, i // nst, 0)

    def resident(a):
        return _layer_spec(a, layer, pipeline_mode=pl.Buffered(1))

    in_specs = [
        pl.BlockSpec((ts, D_MODEL), tok),
        resident(p["norm_g"]), resident(p["w_merge"]), resident(p["b_merge"]),
        pl.BlockSpec((ts, A_WIDTH), tok),
        pl.BlockSpec((ts, B_WIDTH), tok),
        pl.BlockSpec((ts, C_WIDTH), seq_major),
        pl.BlockSpec((ts, M_WIDTH), tok),
        pl.BlockSpec((ts, M_WIDTH), tok),
        pl.BlockSpec((None, M_HEADS * mem_len, M_WIDTH), per_batch),
        pl.BlockSpec((None, M_HEADS * mem_len, M_WIDTH), per_batch),
        resident(p["m_gq"]), resident(p["w_br_a"]), resident(p["w_br_b"]),
        resident(p["w_br_c"]), resident(p["w_br_m"]), resident(p["w_out"]),
    ]
    return pl.pallas_call(
        _merge_step,
        out_shape=jax.ShapeDtypeStruct((n_tok, D_MODEL), F32),
        grid=(n_tok // ts,),
        in_specs=in_specs,
        out_specs=pl.BlockSpec((ts, D_MODEL), tok),
        compiler_params=pltpu.CompilerParams(
            dimension_semantics=("parallel",), vmem_limit_bytes=V7X_VMEM_LIMIT_BYTES),
        name="merge",
    )(x2, p["norm_g"], p["w_merge"], p["b_merge"], ya, yb, yc, mq, mg, mk, mv,
      p["m_gq"], p["w_br_a"], p["w_br_b"], p["w_br_c"], p["w_br_m"], p["w_out"])


W_IN_KROPE = 1792
W_IN_MERGE = 3360
CAST_ROWS = HEAD_PAD
MERGE_CAST_ROWS = 1024
SMALL_SEGMENTS = ((COL_CKV, 1536, B_KV_RANK), (COL_CQ, 768, B_Q_RANK), (COL_AV, 256, A_WIDTH),
                  (COL_AU, 0, A_WIDTH), (COL_AG, 512, A_WIDTH),
                  (COL_BG, W_IN_KROPE + B_ROPE, N_SMALL - COL_BG))


def _cast_rows_body(w_ref, o_ref):
    o_ref[...] = w_ref[...].astype(BF16)


def _cast_small_body(src_ref, w_ref, o_ref):
    del src_ref
    is_kpe = pl.program_id(1) == COL_KPE // CAST_ROWS

    @pl.when(jnp.logical_not(is_kpe))
    def _():
        o_ref[...] = w_ref[...].astype(BF16)

    @pl.when(is_kpe)
    def _():
        o_ref[...] = jnp.zeros(o_ref.shape, BF16)
        o_ref[ROPE_LO:ROPE_LO + ROPE_HALF, :] = w_ref[:ROPE_HALF, :].astype(BF16)
        o_ref[ROPE_HI:ROPE_HI + ROPE_HALF, :] = w_ref[ROPE_HALF:B_ROPE, :].astype(BF16)


def _cast_in_weights(w_in):
    depth, d_model, in_width = w_in.shape
    w_t = jnp.swapaxes(w_in, 1, 2).reshape(depth * in_width, d_model)
    n_merge = N_BRANCH * D_MODEL
    n_blk = n_merge // MERGE_CAST_ROWS
    w_merge_t = pl.pallas_call(
        _cast_rows_body,
        out_shape=jax.ShapeDtypeStruct((depth * n_merge, d_model), BF16),
        grid=(depth, n_blk),
        in_specs=[pl.BlockSpec(
            (pl.Element(MERGE_CAST_ROWS), pl.Element(d_model)),
            lambda l, r: ((l * (in_width // B_ROPE) + W_IN_MERGE // B_ROPE
                           + r * (MERGE_CAST_ROWS // B_ROPE)) * B_ROPE, 0))],
        out_specs=pl.BlockSpec((MERGE_CAST_ROWS, d_model), lambda l, r: (l * n_blk + r, 0)),
        compiler_params=pltpu.CompilerParams(dimension_semantics=("parallel", "parallel")),
        name="cast_merge_weights",
    )(w_t)

    n_blk = N_SMALL // CAST_ROWS
    src_rows = [0] * n_blk
    for dst, src, rows in SMALL_SEGMENTS:
        for b in range(rows // CAST_ROWS):
            src_rows[dst // CAST_ROWS + b] = src + b * CAST_ROWS
    src_rows[COL_KPE // CAST_ROWS] = W_IN_KROPE
    w_small_t = pl.pallas_call(
        _cast_small_body,
        out_shape=jax.ShapeDtypeStruct((depth * N_SMALL, d_model), BF16),
        grid_spec=pltpu.PrefetchScalarGridSpec(
            num_scalar_prefetch=1,
            grid=(depth, n_blk),
            in_specs=[pl.BlockSpec(
                (pl.Element(CAST_ROWS), pl.Element(d_model)),
                lambda l, r, src: ((l * (in_width // B_ROPE) + src[r]) * B_ROPE, 0))],
            out_specs=pl.BlockSpec((CAST_ROWS, d_model), lambda l, r, src: (l * n_blk + r, 0))),
        compiler_params=pltpu.CompilerParams(dimension_semantics=("parallel", "parallel")),
        name="cast_branch_weights",
    )(jnp.asarray(src_rows, jnp.int32) // B_ROPE, w_t)
    return (w_small_t.reshape(depth, N_SMALL, d_model),
            w_merge_t.reshape(depth, n_merge, d_model))


def _row(v):
    return v.reshape(1, -1).astype(F32)


def _prep_layer(norm_g, b_merge, a_norm_g, a_w_s, a_b_s, b_q_norm_g, b_kv_norm_g,
                b_w_uq, b_w_ukv, b_qk_g_q, b_qk_g_k, c_a_re, c_a_im, c_log_dt, c_b_re,
                c_b_im, c_c_re, c_c_im, c_d, c_w_glu, c_b_glu, m_norm_g, m_w_kv, m_qk_g_q,
                m_qk_g_k, w_br_a, w_br_b, w_br_c, w_br_m, w_out):
    p = {}
    p["norm_g"] = _row(norm_g)
    p["b_merge"] = _row(b_merge)

    p["a_norm_g"] = _row(a_norm_g)
    causal = jnp.tril(jnp.ones((A_CHUNK, A_CHUNK), dtype=bool))
    p["a_wst"] = jnp.where(causal, a_w_s, 0.0).reshape(A_GROUPS * A_CHUNK, A_CHUNK).astype(BF16)
    p["a_bias"] = jnp.repeat(a_b_s.T, A_GROUP_DIM, axis=1).astype(F32)

    p["b_q_norm_g"] = _row(b_q_norm_g)
    p["b_kv_norm_g"] = _row(b_kv_norm_g)
    wuq = _head_layout(b_w_uq.reshape(B_Q_RANK, B_HEADS, B_QK_DIM))
    p["b_wuqt"] = wuq.reshape(B_Q_RANK, QK_PAD).T.astype(BF16)
    wukv = b_w_ukv.reshape(B_KV_RANK, B_HEADS, B_NOPE + B_VDIM)
    wuk = _head_layout(jnp.pad(wukv[:, :, :B_NOPE], ((0, 0), (0, 0), (0, B_ROPE))))
    p["b_wuk"] = wuk.reshape(B_KV_RANK, QK_PAD).astype(BF16)
    wuv = jnp.pad(wukv[:, :, B_NOPE:], ((0, 0), (0, 0), (0, V_AUG - B_VDIM)))
    p["b_wuvt"] = wuv.reshape(B_KV_RANK, VT_ROWS).T.astype(BF16)
    p["b_gq"] = jnp.broadcast_to(_head_layout(b_qk_g_q.astype(F32))[:, None],
                                 (HEAD_PAD, TOKEN_TILE))
    p["b_gk"] = _row(_head_layout(b_qk_g_k))

    def rep(a):
        return jnp.repeat(a.astype(F32), C_GROUP, axis=0)

    def gcp(b):
        return b.astype(F32).transpose(0, 2, 1).reshape(C_WIDTH, C_STATE)

    log_dt = jnp.broadcast_to(c_log_dt.astype(F32)[:, None], (C_GROUPS, C_STATE))
    abar_re, abar_im, bbar_re, bbar_im = _s5_discretise(
        rep(c_a_re), rep(c_a_im), rep(log_dt), gcp(c_b_re), gcp(c_b_im))
    eye = jnp.eye(C_GROUPS, dtype=F32)

    def expand_in(m):
        m = m.reshape(C_GROUPS, C_GROUP, C_STATE)
        return (eye[:, None, :, None] * m[:, :, None, :]).reshape(C_WIDTH, C_STATES)

    def expand_out(m):
        return (eye[:, None, :, None] * m.transpose(0, 2, 1)[:, :, None, :]).reshape(
            C_STATES, C_WIDTH)

    n_lb = C_STATES // S5_LANE_BLOCK
    bmat = jnp.stack([expand_in(bbar_re).reshape(C_WIDTH, n_lb, S5_LANE_BLOCK),
                      expand_in(bbar_im).reshape(C_WIDTH, n_lb, S5_LANE_BLOCK)], axis=2)
    p["c_bmat"] = bmat.reshape(C_WIDTH, 2 * C_STATES).astype(BF16)
    cmat = jnp.stack([expand_out(c_c_re.astype(F32)).reshape(n_lb, S5_LANE_BLOCK, C_WIDTH),
                      -expand_out(c_c_im.astype(F32)).reshape(n_lb, S5_LANE_BLOCK, C_WIDTH)],
                     axis=1)
    p["c_cmat"] = cmat.reshape(2 * C_STATES, C_WIDTH).astype(BF16)
    p["c_are"] = _row(abar_re.reshape(C_GROUPS, C_GROUP, C_STATE)[:, 0, :])
    p["c_aim"] = _row(abar_im.reshape(C_GROUPS, C_GROUP, C_STATE)[:, 0, :])
    p["c_d"] = _row(c_d)
    p["c_w_glu"] = c_w_glu.astype(BF16)
    p["c_b_glu"] = _row(c_b_glu)

    p["m_norm_g"] = _row(m_norm_g)
    p["m_w_kv"] = m_w_kv.astype(BF16)
    p["m_gq"] = _row(jnp.tile(m_qk_g_q, M_HEADS)) * (M_HEAD_DIM ** -0.5)
    p["m_gk"] = _row(jnp.tile(m_qk_g_k, M_HEADS))
    p["w_br_a"] = w_br_a.astype(BF16)
    p["w_br_b"] = w_br_b.astype(BF16)
    p["w_br_c"] = w_br_c.astype(BF16)
    p["w_br_m"] = w_br_m.astype(BF16)
    p["w_out"] = w_out.astype(BF16)
    return p


def kernel(x, mem, positions, norm_g, w_in, b_merge, a_norm_g, a_w_s, a_b_s, b_q_norm_g, b_kv_norm_g, b_w_uq, b_w_ukv, b_qk_g_q, b_qk_g_k, c_a_re, c_a_im, c_log_dt, c_b_re, c_b_im, c_c_re, c_c_im, c_d, c_w_glu, c_b_glu, m_norm_g, m_w_kv, m_qk_g_q, m_qk_g_k, w_br_a, w_br_b, w_br_c, w_br_m, w_out):
    bsz, seq, d_model = x.shape
    mem_len = mem.shape[1]
    depth = norm_g.shape[0]
    assert d_model == D_MODEL and seq % (SUB_TILES * TOKEN_TILE) == 0
    assert seq % S5_CHUNK == 0 and bsz == 8 and TOKEN_TILE == ATTN_TILE

    stacked = (norm_g, b_merge, a_norm_g, a_w_s, a_b_s, b_q_norm_g, b_kv_norm_g,
               b_w_uq, b_w_ukv, b_qk_g_q, b_qk_g_k, c_a_re, c_a_im, c_log_dt, c_b_re,
               c_b_im, c_c_re, c_c_im, c_d, c_w_glu, c_b_glu, m_norm_g, m_w_kv, m_qk_g_q,
               m_qk_g_k, w_br_a, w_br_b, w_br_c, w_br_m, w_out)

    tabs = _rope_tables(positions)
    x2 = x.reshape(bsz * seq, d_model)
    mem2 = mem.reshape(bsz * mem_len, d_model)
    p = jax.vmap(_prep_layer)(*stacked)
    p["w_small"], p["w_merge"] = _cast_in_weights(w_in)
    mk, mv = _memkv(mem2, p, bsz, mem_len)
    for layer in range(depth):
        ya, q, k, vt, bg, ci, cg, mq, mg = _inproj(x2, p, layer, tabs, bsz, seq)
        yb = _attention(q, k, vt, bg, bsz, seq)
        yc = _s5(ci, cg, p, layer, bsz, seq)
        x2 = _merge(x2, ya, yb, yc, mq, mg, mk, mv, p, layer, bsz, seq, mem_len)
    return x2.reshape(bsz, seq, d_model)
```

```python
import functools
import math

import jax
import jax.numpy as jnp
from jax import lax
from jax.experimental import pallas as pl
from jax.experimental.pallas import tpu as pltpu

F32 = jnp.float32
BF16 = jnp.bfloat16

D_MODEL = 1024
EPS = 1e-6
N_BRANCH = 4

A_WIDTH = 256
A_GROUPS = 4
A_GROUP_DIM = A_WIDTH // A_GROUPS
A_CHUNK = 128

B_HEADS = 8
B_NOPE = 64
B_ROPE = 32
B_QK_DIM = B_NOPE + B_ROPE
B_VDIM = 64
B_Q_RANK = 768
B_KV_RANK = 256
B_WIDTH = B_HEADS * B_VDIM
ROPE_THETA = 10000.0
HEAD_PAD = 128
QK_PAD = B_HEADS * HEAD_PAD
V_AUG = B_VDIM + 16
VT_ROWS = B_HEADS * V_AUG

C_WIDTH = 256
C_GROUP = 16
C_GROUPS = C_WIDTH // C_GROUP
C_STATE = 64
C_STATES = C_GROUPS * C_STATE

M_HEADS = 4
M_HEAD_DIM = 64
M_WIDTH = M_HEADS * M_HEAD_DIM

COL_CKV = 0
COL_KPE = 256
COL_CQ = 384
COL_AV = 1152
COL_AU = 1408
COL_AG = 1664
COL_BG = 1920
COL_CI = 2432
COL_CG = 2688
COL_MQ = 2944
COL_MG = 3200
N_SMALL = 3456

V7X_VMEM_LIMIT_BYTES = 56 * 1024 * 1024

TOKEN_TILE = 256
SUB_TILES = 2
ATTN_TILE = 256
S5_CHUNK = 64
S5_LANE_BLOCK = 256


def _sigmoid(x):
    return 1.0 / (1.0 + jnp.exp(-x))


def _silu(x):
    return x * _sigmoid(x)


def _rms(x, g):
    return x * lax.rsqrt(jnp.mean(x * x, axis=-1, keepdims=True) + EPS) * g


def _dot(a, b):
    return jnp.dot(a, b, preferred_element_type=F32)


def _dot_nt(a, b):
    return lax.dot_general(a, b, (((1,), (1,)), ((), ())), preferred_element_type=F32)


def _layer_spec(a, layer, **kwargs):
    zeros = (0,) * (a.ndim - 1)
    return pl.BlockSpec((None,) + a.shape[1:], lambda *_: (layer,) + zeros, **kwargs)


ROPE_HALF = B_ROPE // 2
ROPE_LO = 0
ROPE_HI = HEAD_PAD // 2


def _head_layout(w):
    nope, rope = w[..., :B_NOPE], w[..., B_NOPE:]
    split = ROPE_HI - ROPE_HALF
    pad = jnp.zeros(w.shape[:-1] + (HEAD_PAD - B_QK_DIM,), w.dtype)
    return jnp.concatenate([rope[..., :ROPE_HALF], nope[..., :split], rope[..., ROPE_HALF:],
                            nope[..., split:], pad], axis=-1)


def _rope_table_body(pos_row_ref, freq_col_ref, c_ref, s_ref, ct_ref, st_ref):
    ang_t = freq_col_ref[...] * pos_row_ref[...]
    cos_t = jnp.cos(ang_t)
    sin_t = jnp.sin(ang_t)
    ct_ref[...] = cos_t
    st_ref[...] = sin_t
    gap = ROPE_HI - ROPE_HALF
    ones = jnp.ones((gap, ang_t.shape[1]), F32)
    zeros = jnp.zeros((gap, ang_t.shape[1]), F32)
    c_ref[...] = jnp.concatenate([cos_t, ones, cos_t, ones], axis=0).T
    s_ref[...] = jnp.concatenate([-sin_t, zeros, sin_t, zeros], axis=0).T


def _rope_tables(positions):
    n_tok = positions.size
    inv_freq = ROPE_THETA ** (-jnp.arange(ROPE_HALF, dtype=F32) / ROPE_HALF)
    tm = 1024
    freq_col = jnp.broadcast_to(inv_freq[:, None], (ROPE_HALF, tm))
    pos = positions.astype(F32)
    tab = jax.ShapeDtypeStruct((n_tok, HEAD_PAD), F32)
    tab_t = jax.ShapeDtypeStruct((ROPE_HALF, n_tok), F32)
    return pl.pallas_call(
        _rope_table_body,
        out_shape=(tab, tab, tab_t, tab_t),
        grid=(n_tok // tm,),
        in_specs=[pl.BlockSpec((1, tm), lambda i: (0, i)),
                  pl.BlockSpec((ROPE_HALF, tm), lambda i: (0, 0))],
        out_specs=(pl.BlockSpec((tm, HEAD_PAD), lambda i: (i, 0)),) * 2
        + (pl.BlockSpec((ROPE_HALF, tm), lambda i: (0, i)),) * 2,
        compiler_params=pltpu.CompilerParams(dimension_semantics=("parallel",)),
        name="rope_tables",
    )(pos.reshape(1, n_tok), freq_col)


def _inproj_body(x_ref, ng_ref, w_ref, ang_ref, wst_ref, bias_ref, qng_ref, kvng_ref,
                 wuqt_ref, wuk_ref, wuvt_ref, gq_ref, gk_ref, c_ref, s_ref, ct_ref, st_ref,
                 ya_ref, qt_ref, k_ref, vt_ref, bg_ref, ci_ref, cg_ref, mq_ref, mg_ref):
    tm = x_ref.shape[0]
    h = _rms(x_ref[...], ng_ref[...]).astype(BF16)

    z_lat = _dot_nt(h, w_ref[:COL_AV, :])

    def lat(lo, width):
        return z_lat[:, lo:lo + width]

    ckv = _rms(lat(COL_CKV, B_KV_RANK), kvng_ref[...]).astype(BF16)
    kf = _dot(ckv, wuk_ref[...])
    z_rest = _dot_nt(h, w_ref[COL_AV:, :])

    def proj(lo, width):
        return z_rest[:, lo - COL_AV:lo - COL_AV + width]

    kpe = lat(COL_KPE, HEAD_PAD)
    c_tab, s_tab = c_ref[...], s_ref[...]
    for hd in range(B_HEADS):
        lanes = slice(hd * HEAD_PAD, (hd + 1) * HEAD_PAD)
        kh = kf[:, lanes] + kpe
        ssq = jnp.sum(kh * kh, axis=-1, keepdims=True)
        kh = kh * lax.rsqrt(ssq * (1.0 / B_QK_DIM) + EPS) * gk_ref[...]
        k_ref[:, lanes] = (kh * c_tab + pltpu.roll(kh, HEAD_PAD // 2, 1) * s_tab).astype(BF16)
    vt = _dot_nt(wuvt_ref[...], ckv)
    ones_row = lax.broadcasted_iota(jnp.int32, vt.shape, 0) % V_AUG == B_VDIM
    vt_ref[...] = jnp.where(ones_row, 1.0, vt).astype(BF16)

    u = jax.nn.gelu(proj(COL_AU, A_WIDTH))
    v = _rms(jax.nn.gelu(proj(COL_AV, A_WIDTH)), ang_ref[...]).astype(BF16)
    ag = _silu(proj(COL_AG, A_WIDTH))
    lane_group = lax.broadcasted_iota(jnp.int32, (A_CHUNK, A_WIDTH), 1) // A_GROUP_DIM
    for c in range(tm // A_CHUNK):
        rows = slice(c * A_CHUNK, (c + 1) * A_CHUNK)
        s_all = _dot(wst_ref[...], v[rows, :])
        s = s_all[(A_GROUPS - 1) * A_CHUNK:, :]
        for g in range(A_GROUPS - 2, -1, -1):
            s = jnp.where(lane_group == g, s_all[g * A_CHUNK:(g + 1) * A_CHUNK, :], s)
        s = s + bias_ref[...]
        ya_ref[rows, :] = (u[rows, :] * s * ag[rows, :]).astype(BF16)

    q_scale = B_QK_DIM ** -0.5 * math.log2(math.e)
    cq = _rms(lat(COL_CQ, B_Q_RANK), qng_ref[...]).astype(BF16)
    half_rows = QK_PAD // 2
    qt_halves = [_dot_nt(wuqt_ref[i * half_rows:(i + 1) * half_rows, :], cq) for i in range(2)]
    cos_t, sin_t = ct_ref[...], st_ref[...]
    for hd in range(B_HEADS):
        r0 = hd * HEAD_PAD
        qr = r0 % half_rows
        qh = qt_halves[r0 // half_rows][qr:qr + HEAD_PAD, :]
        ssq = jnp.sum(qh * qh, axis=0, keepdims=True)
        qh = qh * (lax.rsqrt(ssq * (1.0 / B_QK_DIM) + EPS) * q_scale) * gq_ref[...]
        x1 = qh[ROPE_LO:ROPE_LO + ROPE_HALF, :]
        x2 = qh[ROPE_HI:ROPE_HI + ROPE_HALF, :]
        qt_ref[r0:r0 + HEAD_PAD, :] = qh.astype(BF16)
        qt_ref[r0 + ROPE_LO:r0 + ROPE_LO + ROPE_HALF, :] = (x1 * cos_t - x2 * sin_t).astype(BF16)
        qt_ref[r0 + ROPE_HI:r0 + ROPE_HI + ROPE_HALF, :] = (x2 * cos_t + x1 * sin_t).astype(BF16)

    bg_ref[...] = _silu(proj(COL_BG, B_WIDTH))
    ci_ref[...] = proj(COL_CI, C_WIDTH)
    cg_ref[...] = _silu(proj(COL_CG, C_WIDTH))
    mq_ref[...] = proj(COL_MQ, M_WIDTH)
    mg_ref[...] = _silu(proj(COL_MG, M_WIDTH))


def _inproj_step(x_ref, *refs):
    weights = refs[:12]
    c_ref, s_ref, ct_ref, st_ref = refs[12:16]
    ya_ref, qt_ref, k_ref, vt_ref, bg_ref, ci_ref, cg_ref, mq_ref, mg_ref = refs[16:]
    tm = TOKEN_TILE
    for sub in range(SUB_TILES):
        rows = pl.ds(sub * tm, tm)
        _inproj_body(x_ref.at[rows], *weights, c_ref.at[rows], s_ref.at[rows],
                     ct_ref.at[:, rows], st_ref.at[:, rows],
                     ya_ref.at[rows], qt_ref.at[sub], k_ref.at[rows], vt_ref.at[sub],
                     bg_ref.at[rows], ci_ref.at[rows], cg_ref.at[rows], mq_ref.at[rows],
                     mg_ref.at[rows])


def _inproj(x2, p, layer, tabs, bsz, seq):
    n_tok = x2.shape[0]
    tm = TOKEN_TILE
    ts = SUB_TILES * tm
    nst = seq // ts

    def tok(i):
        return (i, 0)

    def seq_major(i):
        return (i % nst, i // nst)

    weights = (p["norm_g"], p["w_small"], p["a_norm_g"], p["a_wst"], p["a_bias"],
               p["b_q_norm_g"], p["b_kv_norm_g"], p["b_wuqt"], p["b_wuk"], p["b_wuvt"],
               p["b_gq"], p["b_gk"])
    in_specs = ([pl.BlockSpec((ts, D_MODEL), tok)] + [_layer_spec(w, layer) for w in weights]
                + [pl.BlockSpec((ts, HEAD_PAD), tok)] * 2
                + [pl.BlockSpec((ROPE_HALF, ts), lambda i: (0, i))] * 2)
    out_shape = (
        jax.ShapeDtypeStruct((n_tok, A_WIDTH), BF16),
        jax.ShapeDtypeStruct((n_tok // tm, QK_PAD, tm), BF16),
        jax.ShapeDtypeStruct((n_tok, QK_PAD), BF16),
        jax.ShapeDtypeStruct((n_tok // tm, VT_ROWS, tm), BF16),
        jax.ShapeDtypeStruct((n_tok, B_WIDTH), F32),
        jax.ShapeDtypeStruct((seq, bsz * C_WIDTH), F32),
        jax.ShapeDtypeStruct((seq, bsz * C_WIDTH), F32),
        jax.ShapeDtypeStruct((n_tok, M_WIDTH), F32),
        jax.ShapeDtypeStruct((n_tok, M_WIDTH), F32),
    )
    out_specs = (
        pl.BlockSpec((ts, A_WIDTH), tok),
        pl.BlockSpec((SUB_TILES, QK_PAD, tm), lambda i: (i, 0, 0)),
        pl.BlockSpec((ts, QK_PAD), tok),
        pl.BlockSpec((SUB_TILES, VT_ROWS, tm), lambda i: (i, 0, 0)),
        pl.BlockSpec((ts, B_WIDTH), tok),
        pl.BlockSpec((ts, C_WIDTH), seq_major),
        pl.BlockSpec((ts, C_WIDTH), seq_major),
        pl.BlockSpec((ts, M_WIDTH), tok),
        pl.BlockSpec((ts, M_WIDTH), tok),
    )
    return pl.pallas_call(
        _inproj_step,
        out_shape=out_shape,
        grid=(n_tok // ts,),
        in_specs=in_specs,
        out_specs=out_specs,
        compiler_params=pltpu.CompilerParams(
            dimension_semantics=("parallel",), vmem_limit_bytes=V7X_VMEM_LIMIT_BYTES),
        name="inproj",
    )(x2, *weights, *tabs)


def _attn_body(qt_ref, k_ref, vt_ref, g_ref, o_ref, m_ref, a_ref, acc_ref, ot_ref, s_ref):
    tq = qt_ref.shape[1]
    qi = pl.program_id(1)
    key_le_query = (lax.broadcasted_iota(jnp.int32, (tq, tq), 0)
                    <= lax.broadcasted_iota(jnp.int32, (tq, tq), 1))
    m_ref[1] = jnp.full(m_ref.shape[1:], -jnp.inf, F32)
    acc_ref[...] = jnp.zeros(acc_ref.shape, F32)

    def scores_head(j, masked, slot, hd):
        r0 = pl.multiple_of(j * tq, tq)
        lanes = slice(hd * HEAD_PAD, (hd + 1) * HEAD_PAD)
        st = _dot(k_ref[pl.ds(r0, tq), lanes], qt_ref[lanes, :])
        if masked:
            st = jnp.where(key_le_query, st, -jnp.inf)
        s_ref[slot, hd] = st
        m_old = m_ref[1 - slot, hd:hd + 1, :]
        m_new = jnp.maximum(m_old, jnp.max(st, axis=0, keepdims=True))
        m_ref[slot, hd:hd + 1, :] = m_new
        a_ref[slot, hd:hd + 1, :] = jnp.exp2(m_old - m_new)

    def accumulate_head(j, slot, hd):
        rows = slice(hd * V_AUG, (hd + 1) * V_AUG)
        pt = jnp.exp2(s_ref[slot, hd] - m_ref[slot, hd:hd + 1, :])
        acc_ref[rows, :] = (a_ref[slot, hd:hd + 1, :] * acc_ref[rows, :]
                            + _dot(vt_ref[j, rows, :], pt.astype(BF16)))

    def scores(j, masked, slot):
        for hd in range(B_HEADS):
            scores_head(j, masked, slot, hd)

    def accumulate(j, slot):
        for hd in range(B_HEADS):
            accumulate_head(j, slot, hd)

    def scores_and_accumulate(j, masked, slot):
        for hd in range(B_HEADS):
            scores_head(j, masked, slot, hd)
            accumulate_head(j - 1, 1 - slot, hd)

    @pl.when(qi == 0)
    def _():
        scores(0, True, 0)
        accumulate(0, 0)

    @pl.when(qi > 0)
    def _():
        scores(0, False, 0)

        def two_blocks(i, carry):
            scores_and_accumulate(2 * i + 1, False, 1)
            scores_and_accumulate(2 * i + 2, False, 0)
            return carry

        lax.fori_loop(0, (qi - 1) // 2, two_blocks, 0)

        @pl.when(qi % 2 == 0)
        def _():
            scores_and_accumulate(qi - 1, False, 1)
            scores_and_accumulate(qi, True, 0)
            accumulate(qi, 0)

        @pl.when(qi % 2 == 1)
        def _():
            scores_and_accumulate(qi, True, 1)
            accumulate(qi, 1)

    for hd in range(B_HEADS):
        r0 = hd * V_AUG
        denom = acc_ref[r0 + B_VDIM:r0 + B_VDIM + 1, :]
        ot_ref[hd * B_VDIM:(hd + 1) * B_VDIM, :] = acc_ref[r0:r0 + B_VDIM, :] * (1.0 / denom)
    o_ref[...] = (ot_ref[...].T * g_ref[...]).astype(BF16)


def _attention(q, k, vt, gate, bsz, seq):
    tq = ATTN_TILE
    nq = seq // tq
    return pl.pallas_call(
        _attn_body,
        out_shape=jax.ShapeDtypeStruct((bsz * seq, B_WIDTH), BF16),
        grid=(bsz, nq),
        in_specs=[pl.BlockSpec((None, QK_PAD, tq), lambda b, i: (b * nq + i, 0, 0)),
                  pl.BlockSpec((seq, QK_PAD), lambda b, i: (b, 0)),
                  pl.BlockSpec((nq, VT_ROWS, tq), lambda b, i: (b, 0, 0)),
                  pl.BlockSpec((tq, B_WIDTH), lambda b, i: (b * nq + i, 0))],
        out_specs=pl.BlockSpec((tq, B_WIDTH), lambda b, i: (b * nq + i, 0)),
        scratch_shapes=[pltpu.VMEM((2, B_HEADS, tq), F32), pltpu.VMEM((2, B_HEADS, tq), F32),
                        pltpu.VMEM((VT_ROWS, tq), F32), pltpu.VMEM((B_WIDTH, tq), F32),
                        pltpu.VMEM((2, B_HEADS, tq, tq), F32)],
        compiler_params=pltpu.CompilerParams(
            dimension_semantics=("parallel", "parallel"),
            vmem_limit_bytes=V7X_VMEM_LIMIT_BYTES),
        name="latent_attention",
    )(q, k, vt, gate)


def _s5_discretise_body(are_ref, aim_ref, ldt_ref, bre_ref, bim_ref,
                        abar_re_ref, abar_im_ref, bbar_re_ref, bbar_im_ref):
    a_re, a_im = are_ref[...], aim_ref[...]
    dt = jnp.exp(ldt_ref[...])
    mag = jnp.exp(a_re * dt)
    abar_re = mag * jnp.cos(a_im * dt)
    abar_im = mag * jnp.sin(a_im * dt)
    num_re = abar_re - 1.0
    inv_den = 1.0 / (a_re * a_re + a_im * a_im)
    q_re = (num_re * a_re + abar_im * a_im) * inv_den
    q_im = (abar_im * a_re - num_re * a_im) * inv_den
    b_re, b_im = bre_ref[...], bim_ref[...]
    abar_re_ref[...] = abar_re
    abar_im_ref[...] = abar_im
    bbar_re_ref[...] = q_re * b_re - q_im * b_im
    bbar_im_ref[...] = q_re * b_im + q_im * b_re


def _s5_discretise(a_re, a_im, log_dt, b_re, b_im):
    out = jax.ShapeDtypeStruct(a_re.shape, F32)
    return pl.pallas_call(_s5_discretise_body, out_shape=(out,) * 4, name="s5_discretise")(
        a_re, a_im, log_dt, b_re, b_im)


def _s5_body(u_ref, cg_ref, bmat_ref, cmat_ref, are_ref, aim_ref, d_ref, wglu_ref, bglu_ref,
             y_ref, state_ref, buf_ref):
    lt, bsz, width = u_ref.shape
    rows = lt * bsz
    lb_w = S5_LANE_BLOCK

    @pl.when(pl.program_id(0) == 0)
    def _():
        state_ref[...] = jnp.zeros_like(state_ref)

    u2 = u_ref[...].reshape(rows, width)
    ub = u2.astype(BF16)
    half = rows // 2
    y_halves = [None, None]
    n_lb = C_STATES // lb_w

    def input_matmul(lb):
        cols = slice(2 * lb * lb_w, 2 * (lb + 1) * lb_w)
        buf_ref[:, cols] = _dot(ub, bmat_ref[:, cols])

    input_matmul(0)
    for lb in range(n_lb):
        cols = slice(2 * lb * lb_w, 2 * (lb + 1) * lb_w)
        re_l = slice(2 * lb * lb_w, (2 * lb + 1) * lb_w)
        im_l = slice((2 * lb + 1) * lb_w, 2 * (lb + 1) * lb_w)
        if lb + 1 < n_lb:
            input_matmul(lb + 1)
        a_re = jnp.broadcast_to(are_ref[:, lb * lb_w:(lb + 1) * lb_w], (bsz, lb_w))
        a_im = jnp.broadcast_to(aim_ref[:, lb * lb_w:(lb + 1) * lb_w], (bsz, lb_w))
        s_re, s_im = state_ref[:, re_l], state_ref[:, im_l]
        for t in range(lt):
            r = slice(t * bsz, (t + 1) * bsz)
            n_re = a_re * s_re - a_im * s_im + buf_ref[r, re_l]
            n_im = a_re * s_im + a_im * s_re + buf_ref[r, im_l]
            buf_ref[r, re_l] = n_re
            buf_ref[r, im_l] = n_im
            s_re, s_im = n_re, n_im
        state_ref[:, re_l] = s_re
        state_ref[:, im_l] = s_im
        for hf in range(2):
            r = slice(hf * half, (hf + 1) * half)
            part = _dot(buf_ref[r, cols].astype(BF16), cmat_ref[cols, :])
            y_halves[hf] = part if y_halves[hf] is None else y_halves[hf] + part

    y = jnp.concatenate(y_halves, axis=0) + d_ref[...] * u2
    y = jax.nn.gelu(y)
    y = y * _sigmoid(_dot(y.astype(BF16), wglu_ref[...]) + bglu_ref[...])
    y_ref[...] = (y * cg_ref[...].reshape(rows, width)).reshape(lt, bsz, width)


def _s5(c_in, c_gate, p, layer, bsz, seq):
    lt = S5_CHUNK
    u3 = c_in.reshape(seq, bsz, C_WIDTH)
    g3 = c_gate.reshape(seq, bsz, C_WIDTH)

    def chunk(i):
        return (i, 0, 0)

    weights = (p["c_bmat"], p["c_cmat"], p["c_are"], p["c_aim"], p["c_d"], p["c_w_glu"],
               p["c_b_glu"])
    y = pl.pallas_call(
        _s5_body,
        out_shape=jax.ShapeDtypeStruct((seq, bsz, C_WIDTH), F32),
        grid=(seq // lt,),
        in_specs=[pl.BlockSpec((lt, bsz, C_WIDTH), chunk)] * 2
        + [_layer_spec(w, layer) for w in weights],
        out_specs=pl.BlockSpec((lt, bsz, C_WIDTH), chunk),
        scratch_shapes=[pltpu.VMEM((bsz, 2 * C_STATES), F32),
                        pltpu.VMEM((lt * bsz, 2 * C_STATES), F32)],
        compiler_params=pltpu.CompilerParams(
            dimension_semantics=("arbitrary",), vmem_limit_bytes=V7X_VMEM_LIMIT_BYTES),
        name="s5_scan",
    )(u3, g3, *weights)
    return y.reshape(seq, bsz * C_WIDTH)


def _head_rms(x, gain, n_heads, head_dim):
    head_id = lax.broadcasted_iota(jnp.int32, x.shape, 1) // head_dim
    out = jnp.zeros_like(x)
    for hd in range(n_heads):
        xh = jnp.where(head_id == hd, x, 0.0)
        ssq = jnp.sum(xh * xh, axis=-1, keepdims=True)
        out = out + xh * lax.rsqrt(ssq * (1.0 / head_dim) + EPS)
    return out * gain


def _memkv_body(mem_ref, ng_ref, wkv_ref, gk_ref, k_ref, v_ref):
    mem_len = mem_ref.shape[0]
    mem = mem_ref[...]
    head_id = lax.broadcasted_iota(jnp.int32, (mem_len, M_WIDTH), 1) // M_HEAD_DIM
    for layer in range(ng_ref.shape[0]):
        mh = _rms(mem, ng_ref[layer]).astype(BF16)
        kv = _dot(mh, wkv_ref[layer])
        k = _head_rms(kv[:, :M_WIDTH], gk_ref[layer], M_HEADS, M_HEAD_DIM)
        v = kv[:, M_WIDTH:]
        for hd in range(M_HEADS):
            rows = slice(hd * mem_len, (hd + 1) * mem_len)
            k_ref[layer, rows, :] = jnp.where(head_id == hd, k, 0.0).astype(BF16)
            v_ref[layer, rows, :] = jnp.where(head_id == hd, v, 0.0).astype(BF16)


def _memkv(mem2, p, bsz, mem_len):
    weights = (p["m_norm_g"], p["m_w_kv"], p["m_gk"])
    depth = weights[0].shape[0]
    out = jax.ShapeDtypeStruct((depth, bsz * M_HEADS * mem_len, M_WIDTH), BF16)
    return pl.pallas_call(
        _memkv_body,
        out_shape=(out, out),
        grid=(bsz,),
        in_specs=[pl.BlockSpec((mem_len, D_MODEL), lambda b: (b, 0))]
        + [pl.BlockSpec(w.shape, lambda b: (0, 0, 0)) for w in weights],
        out_specs=(pl.BlockSpec((depth, M_HEADS * mem_len, M_WIDTH), lambda b: (0, b, 0)),) * 2,
        compiler_params=pltpu.CompilerParams(dimension_semantics=("parallel",)),
        name="memory_kv",
    )(mem2, *weights)


def _merge_body(x_ref, ng_ref, wm_ref, bm_ref, ya_ref, yb_ref, yc_ref, mq_ref, mg_ref,
                mk_ref, mv_ref, gq_ref, wa_ref, wb_ref, wc_ref, wmm_ref, wo_ref, o_ref):
    x = x_ref[...]
    h = _rms(x, ng_ref[...]).astype(BF16)

    def gated(br, y, w_ref):
        cols = slice(br * D_MODEL, (br + 1) * D_MODEL)
        gate = _sigmoid(_dot_nt(h, wm_ref[cols, :]) + bm_ref[:, cols])
        return gate * _dot(y, w_ref[...])

    mq = mq_ref[...]
    mem_len = mk_ref.shape[0] // M_HEADS
    same_head = (lax.broadcasted_iota(jnp.int32, (M_WIDTH, M_WIDTH), 0) // M_HEAD_DIM
                 == lax.broadcasted_iota(jnp.int32, (M_WIDTH, M_WIDTH), 1) // M_HEAD_DIM)
    head_ones = jnp.where(same_head, 1.0, 0.0).astype(BF16)
    q_sq = mq * mq
    q_sq_hi = q_sq.astype(BF16)
    q_sq_lo = (q_sq - q_sq_hi.astype(F32)).astype(BF16)
    ssq = _dot(q_sq_hi, head_ones) + _dot(q_sq_lo, head_ones)
    merged = gated(0, ya_ref[...], wa_ref)
    qn = (mq * lax.rsqrt(ssq * (1.0 / M_HEAD_DIM) + EPS) * gq_ref[...]).astype(BF16)
    s_all = _dot_nt(qn, mk_ref[...])
    merged = merged + gated(1, yb_ref[...], wb_ref)
    probs = []
    for hd in range(M_HEADS):
        s = s_all[:, hd * mem_len:(hd + 1) * mem_len]
        e = jnp.exp(s - jnp.max(s, axis=-1, keepdims=True))
        probs.append((e * (1.0 / jnp.sum(e, axis=-1, keepdims=True))).astype(BF16))
    om = _dot(jnp.concatenate(probs, axis=1), mv_ref[...])
    merged = merged + gated(2, yc_ref[...].astype(BF16), wc_ref)
    ym = (om * mg_ref[...]).astype(BF16)
    merged = merged + gated(3, ym, wmm_ref)
    o_ref[...] = x + _dot(merged.astype(BF16), wo_ref[...])


def _merge_step(x_ref, ng_ref, wm_ref, bm_ref, ya_ref, yb_ref, yc_ref, mq_ref, mg_ref,
                mk_ref, mv_ref, gq_ref, wa_ref, wb_ref, wc_ref, wmm_ref, wo_ref, o_ref):
    tm = TOKEN_TILE
    for sub in range(SUB_TILES):
        rows = pl.ds(sub * tm, tm)
        _merge_body(x_ref.at[rows], ng_ref, wm_ref, bm_ref, ya_ref.at[rows], yb_ref.at[rows],
                    yc_ref.at[rows], mq_ref.at[rows], mg_ref.at[rows], mk_ref, mv_ref, gq_ref,
                    wa_ref, wb_ref, wc_ref, wmm_ref, wo_ref, o_ref.at[rows])


def _merge(x2, ya, yb, yc, mq, mg, mk, mv, p, layer, bsz, seq, mem_len):
    n_tok = x2.shape[0]
    ts = SUB_TILES * TOKEN_TILE
    nst = seq // ts

    def tok(i):
        return (i, 0)

    def seq_major(i):
        return (i % nst, i // nst)

    def per_batch(i):
        return (layer, i // nst, 0)

    def resident(a):
        return _layer_spec(a, layer, pipeline_mode=pl.Buffered(1))

    in_specs = [
        pl.BlockSpec((ts, D_MODEL), tok),
        resident(p["norm_g"]), resident(p["w_merge"]), resident(p["b_merge"]),
        pl.BlockSpec((ts, A_WIDTH), tok),
        pl.BlockSpec((ts, B_WIDTH), tok),
        pl.BlockSpec((ts, C_WIDTH), seq_major),
        pl.BlockSpec((ts, M_WIDTH), tok),
        pl.BlockSpec((ts, M_WIDTH), tok),
        pl.BlockSpec((None, M_HEADS * mem_len, M_WIDTH), per_batch),
        pl.BlockSpec((None, M_HEADS * mem_len, M_WIDTH), per_batch),
        resident(p["m_gq"]), resident(p["w_br_a"]), resident(p["w_br_b"]),
        resident(p["w_br_c"]), resident(p["w_br_m"]), resident(p["w_out"]),
    ]
    return pl.pallas_call(
        _merge_step,
        out_shape=jax.ShapeDtypeStruct((n_tok, D_MODEL), F32),
        grid=(n_tok // ts,),
        in_specs=in_specs,
        out_specs=pl.BlockSpec((ts, D_MODEL), tok),
        compiler_params=pltpu.CompilerParams(
            dimension_semantics=("parallel",), vmem_limit_bytes=V7X_VMEM_LIMIT_BYTES),
        name="merge",
    )(x2, p["norm_g"], p["w_merge"], p["b_merge"], ya, yb, yc, mq, mg, mk, mv,
      p["m_gq"], p["w_br_a"], p["w_br_b"], p["w_br_c"], p["w_br_m"], p["w_out"])


W_IN_KROPE = 1792
W_IN_MERGE = 3360
CAST_ROWS = HEAD_PAD
MERGE_CAST_ROWS = 1024
SMALL_SEGMENTS = ((COL_CKV, 1536, B_KV_RANK), (COL_CQ, 768, B_Q_RANK), (COL_AV, 256, A_WIDTH),
                  (COL_AU, 0, A_WIDTH), (COL_AG, 512, A_WIDTH),
                  (COL_BG, W_IN_KROPE + B_ROPE, N_SMALL - COL_BG))


def _cast_rows_body(w_ref, o_ref):
    o_ref[...] = w_ref[...].astype(BF16)


def _cast_small_body(src_ref, w_ref, o_ref):
    del src_ref
    is_kpe = pl.program_id(1) == COL_KPE // CAST_ROWS

    @pl.when(jnp.logical_not(is_kpe))
    def _():
        o_ref[...] = w_ref[...].astype(BF16)

    @pl.when(is_kpe)
    def _():
        o_ref[...] = jnp.zeros(o_ref.shape, BF16)
        o_ref[ROPE_LO:ROPE_LO + ROPE_HALF, :] = w_ref[:ROPE_HALF, :].astype(BF16)
        o_ref[ROPE_HI:ROPE_HI + ROPE_HALF, :] = w_ref[ROPE_HALF:B_ROPE, :].astype(BF16)


def _cast_in_weights(w_in):
    depth, d_model, in_width = w_in.shape
    w_t = jnp.swapaxes(w_in, 1, 2).reshape(depth * in_width, d_model)
    n_merge = N_BRANCH * D_MODEL
    n_blk = n_merge // MERGE_CAST_ROWS
    w_merge_t = pl.pallas_call(
        _cast_rows_body,
        out_shape=jax.ShapeDtypeStruct((depth * n_merge, d_model), BF16),
        grid=(depth, n_blk),
        in_specs=[pl.BlockSpec(
            (pl.Element(MERGE_CAST_ROWS), pl.Element(d_model)),
            lambda l, r: ((l * (in_width // B_ROPE) + W_IN_MERGE // B_ROPE
                           + r * (MERGE_CAST_ROWS // B_ROPE)) * B_ROPE, 0))],
        out_specs=pl.BlockSpec((MERGE_CAST_ROWS, d_model), lambda l, r: (l * n_blk + r, 0)),
        compiler_params=pltpu.CompilerParams(dimension_semantics=("parallel", "parallel")),
        name="cast_merge_weights",
    )(w_t)

    n_blk = N_SMALL // CAST_ROWS
    src_rows = [0] * n_blk
    for dst, src, rows in SMALL_SEGMENTS:
        for b in range(rows // CAST_ROWS):
            src_rows[dst // CAST_ROWS + b] = src + b * CAST_ROWS
    src_rows[COL_KPE // CAST_ROWS] = W_IN_KROPE
    w_small_t = pl.pallas_call(
        _cast_small_body,
        out_shape=jax.ShapeDtypeStruct((depth * N_SMALL, d_model), BF16),
        grid_spec=pltpu.PrefetchScalarGridSpec(
            num_scalar_prefetch=1,
            grid=(depth, n_blk),
            in_specs=[pl.BlockSpec(
                (pl.Element(CAST_ROWS), pl.Element(d_model)),
                lambda l, r, src: ((l * (in_width // B_ROPE) + src[r]) * B_ROPE, 0))],
            out_specs=pl.BlockSpec((CAST_ROWS, d_model), lambda l, r, src: (l * n_blk + r, 0))),
        compiler_params=pltpu.CompilerParams(dimension_semantics=("parallel", "parallel")),
        name="cast_branch_weights",
    )(jnp.asarray(src_rows, jnp.int32) // B_ROPE, w_t)
    return (w_small_t.reshape(depth, N_SMALL, d_model),
            w_merge_t.reshape(depth, n_merge, d_model))


def _row(v):
    return v.reshape(1, -1).astype(F32)


def _prep_layer(norm_g, b_merge, a_norm_g, a_w_s, a_b_s, b_q_norm_g, b_kv_norm_g,
                b_w_uq, b_w_ukv, b_qk_g_q, b_qk_g_k, c_a_re, c_a_im, c_log_dt, c_b_re,
                c_b_im, c_c_re, c_c_im, c_d, c_w_glu, c_b_glu, m_norm_g, m_w_kv, m_qk_g_q,
                m_qk_g_k, w_br_a, w_br_b, w_br_c, w_br_m, w_out):
    p = {}
    p["norm_g"] = _row(norm_g)
    p["b_merge"] = _row(b_merge)

    p["a_norm_g"] = _row(a_norm_g)
    causal = jnp.tril(jnp.ones((A_CHUNK, A_CHUNK), dtype=bool))
    p["a_wst"] = jnp.where(causal, a_w_s, 0.0).reshape(A_GROUPS * A_CHUNK, A_CHUNK).astype(BF16)
    p["a_bias"] = jnp.repeat(a_b_s.T, A_GROUP_DIM, axis=1).astype(F32)

    p["b_q_norm_g"] = _row(b_q_norm_g)
    p["b_kv_norm_g"] = _row(b_kv_norm_g)
    wuq = _head_layout(b_w_uq.reshape(B_Q_RANK, B_HEADS, B_QK_DIM))
    p["b_wuqt"] = wuq.reshape(B_Q_RANK, QK_PAD).T.astype(BF16)
    wukv = b_w_ukv.reshape(B_KV_RANK, B_HEADS, B_NOPE + B_VDIM)
    wuk = _head_layout(jnp.pad(wukv[:, :, :B_NOPE], ((0, 0), (0, 0), (0, B_ROPE))))
    p["b_wuk"] = wuk.reshape(B_KV_RANK, QK_PAD).astype(BF16)
    wuv = jnp.pad(wukv[:, :, B_NOPE:], ((0, 0), (0, 0), (0, V_AUG - B_VDIM)))
    p["b_wuvt"] = wuv.reshape(B_KV_RANK, VT_ROWS).T.astype(BF16)
    p["b_gq"] = jnp.broadcast_to(_head_layout(b_qk_g_q.astype(F32))[:, None],
                                 (HEAD_PAD, TOKEN_TILE))
    p["b_gk"] = _row(_head_layout(b_qk_g_k))

    def rep(a):
        return jnp.repeat(a.astype(F32), C_GROUP, axis=0)

    def gcp(b):
        return b.astype(F32).transpose(0, 2, 1).reshape(C_WIDTH, C_STATE)

    log_dt = jnp.broadcast_to(c_log_dt.astype(F32)[:, None], (C_GROUPS, C_STATE))
    abar_re, abar_im, bbar_re, bbar_im = _s5_discretise(
        rep(c_a_re), rep(c_a_im), rep(log_dt), gcp(c_b_re), gcp(c_b_im))
    eye = jnp.eye(C_GROUPS, dtype=F32)

    def expand_in(m):
        m = m.reshape(C_GROUPS, C_GROUP, C_STATE)
        return (eye[:, None, :, None] * m[:, :, None, :]).reshape(C_WIDTH, C_STATES)

    def expand_out(m):
        return (eye[:, None, :, None] * m.transpose(0, 2, 1)[:, :, None, :]).reshape(
            C_STATES, C_WIDTH)

    n_lb = C_STATES // S5_LANE_BLOCK
    bmat = jnp.stack([expand_in(bbar_re).reshape(C_WIDTH, n_lb, S5_LANE_BLOCK),
                      expand_in(bbar_im).reshape(C_WIDTH, n_lb, S5_LANE_BLOCK)], axis=2)
    p["c_bmat"] = bmat.reshape(C_WIDTH, 2 * C_STATES).astype(BF16)
    cmat = jnp.stack([expand_out(c_c_re.astype(F32)).reshape(n_lb, S5_LANE_BLOCK, C_WIDTH),
                      -expand_out(c_c_im.astype(F32)).reshape(n_lb, S5_LANE_BLOCK, C_WIDTH)],
                     axis=1)
    p["c_cmat"] = cmat.reshape(2 * C_STATES, C_WIDTH).astype(BF16)
    p["c_are"] = _row(abar_re.reshape(C_GROUPS, C_GROUP, C_STATE)[:, 0, :])
    p["c_aim"] = _row(abar_im.reshape(C_GROUPS, C_GROUP, C_STATE)[:, 0, :])
    p["c_d"] = _row(c_d)
    p["c_w_glu"] = c_w_glu.astype(BF16)
    p["c_b_glu"] = _row(c_b_glu)

    p["m_norm_g"] = _row(m_norm_g)
    p["m_w_kv"] = m_w_kv.astype(BF16)
    p["m_gq"] = _row(jnp.tile(m_qk_g_q, M_HEADS)) * (M_HEAD_DIM ** -0.5)
    p["m_gk"] = _row(jnp.tile(m_qk_g_k, M_HEADS))
    p["w_br_a"] = w_br_a.astype(BF16)
    p["w_br_b"] = w_br_b.astype(BF16)
    p["w_br_c"] = w_br_c.astype(BF16)
    p["w_br_m"] = w_br_m.astype(BF16)
    p["w_out"] = w_out.astype(BF16)
    return p


def kernel(x, mem, positions, norm_g, w_in, b_merge, a_norm_g, a_w_s, a_b_s, b_q_norm_g, b_kv_norm_g, b_w_uq, b_w_ukv, b_qk_g_q, b_qk_g_k, c_a_re, c_a_im, c_log_dt, c_b_re, c_b_im, c_c_re, c_c_im, c_d, c_w_glu, c_b_glu, m_norm_g, m_w_kv, m_qk_g_q, m_qk_g_k, w_br_a, w_br_b, w_br_c, w_br_m, w_out):
    bsz, seq, d_model = x.shape
    mem_len = mem.shape[1]
    depth = norm_g.shape[0]
    assert d_model == D_MODEL and seq % (SUB_TILES * TOKEN_TILE) == 0
    assert seq % S5_CHUNK == 0 and bsz == 8 and TOKEN_TILE == ATTN_TILE

    stacked = (norm_g, b_merge, a_norm_g, a_w_s, a_b_s, b_q_norm_g, b_kv_norm_g,
               b_w_uq, b_w_ukv, b_qk_g_q, b_qk_g_k, c_a_re, c_a_im, c_log_dt, c_b_re,
               c_b_im, c_c_re, c_c_im, c_d, c_w_glu, c_b_glu, m_norm_g, m_w_kv, m_qk_g_q,
               m_qk_g_k, w_br_a, w_br_b, w_br_c, w_br_m, w_out)

    tabs = _rope_tables(positions)
    x2 = x.reshape(bsz * seq, d_model)
    mem2 = mem.reshape(bsz * mem_len, d_model)
    p = jax.vmap(_prep_layer)(*stacked)
    p["w_small"], p["w_merge"] = _cast_in_weights(w_in)
    mk, mv = _memkv(mem2, p, bsz, mem_len)
    for layer in range(depth):
        ya, q, k, vt, bg, ci, cg, mq, mg = _inproj(x2, p, layer, tabs, bsz, seq)
        yb = _attention(q, k, vt, bg, bsz, seq)
        yc = _s5(ci, cg, p, layer, bsz, seq)
        x2 = _merge(x2, ya, yb, yc, mq, mg, mk, mv, p, layer, bsz, seq, mem_len)
    return x2.reshape(bsz, seq, d_model)
```
